```python
import math
import jax, jax.numpy as jnp
from jax import lax
import numpy as np

D_MODEL = 1024
BATCH = 2
SEQ = 8192
DEPTH = 2

CHUNK = 64
N_MIXERS = 2
N_A_LAYERS = (DEPTH + 1) // 2
N_B_LAYERS = DEPTH // 2
EPS = 1e-6

RET_HEADS = 4
RET_QK_DIM = D_MODEL // RET_HEADS
RET_V_DIM = 2 * RET_QK_DIM
RET_V_WIDTH = RET_HEADS * RET_V_DIM
RET_IN_WIDTH = 2 * D_MODEL + 2 * RET_V_WIDTH
ROPE_BASE = 10000.0

SGU_BLOCK = 128
SGU_GROUPS = 8
SGU_FFN = 6 * D_MODEL
SGU_HALF = SGU_FFN // 2
SGU_GROUP_DIM = SGU_HALF // SGU_GROUPS

PEER_HEADS = 8
PEER_NKEYS = 128
PEER_EXPERTS = PEER_NKEYS * PEER_NKEYS
PEER_KEY_DIM = 256
PEER_HALF = PEER_KEY_DIM // 2
PEER_TOPK = 16
PEER_TOKEN_BLOCK = 128

kernel_name = "hybrid_retention_sgu_peer_adaln"


def rmsnorm(x, gain):
    x32 = x.astype(jnp.float32)
    y = x32 * lax.rsqrt(jnp.mean(x32 * x32, axis=-1, keepdims=True) + EPS)
    return (y * gain.astype(jnp.float32)).astype(x.dtype)


def layernorm(x, gain, bias):
    x32 = x.astype(jnp.float32)
    mu = jnp.mean(x32, axis=-1, keepdims=True)
    xc = x32 - mu
    y = xc * lax.rsqrt(jnp.mean(xc * xc, axis=-1, keepdims=True) + EPS)
    return (y * gain.astype(jnp.float32) + bias.astype(jnp.float32)).astype(x.dtype)


def modulate(h, shift, scale):
    return h * (1 + scale[:, None, :]) + shift[:, None, :]


def rotary(x, positions):
    half = x.shape[-1] // 2
    inv_freq = 1.0 / (ROPE_BASE ** (jnp.arange(half, dtype=jnp.float32) / half))
    ang = positions.astype(jnp.float32)[..., None] * inv_freq
    cos = jnp.cos(ang)[:, :, None, :]
    sin = jnp.sin(ang)[:, :, None, :]
    x32 = x.astype(jnp.float32)
    x1, x2 = x32[..., :half], x32[..., half:]
    return jnp.concatenate([x1 * cos - x2 * sin, x1 * sin + x2 * cos], axis=-1)


def retention_mixer(h, positions, w_in, w_out):
    B, S, _ = h.shape
    nc = S // CHUNK
    H, Dk, Dv = RET_HEADS, RET_QK_DIM, RET_V_DIM
    proj = h @ w_in
    q, k, v, g = jnp.split(proj, [D_MODEL, 2 * D_MODEL, 2 * D_MODEL + RET_V_WIDTH], axis=-1)
    q = rotary(q.reshape(B, S, H, Dk), positions)
    k = rotary(k.reshape(B, S, H, Dk), positions) * (Dk ** -0.5)
    v = v.reshape(B, S, H, Dv).astype(jnp.float32)
    log_gamma = jnp.log(1.0 - 2.0 ** (-5.0 - jnp.arange(H, dtype=jnp.float32)))
    idx = jnp.arange(CHUNK, dtype=jnp.float32)
    d_intra = jnp.exp(log_gamma[:, None, None] * jnp.abs(idx[:, None] - idx[None, :]))
    xi = jnp.exp(log_gamma[None, :] * (idx[:, None] + 1.0))
    zeta = jnp.exp(log_gamma[None, :] * (CHUNK - 1.0 - idx[:, None]))
    gamma_chunk = jnp.exp(log_gamma * CHUNK)

    def to_chunks(t):
        return jnp.moveaxis(t.reshape(B, nc, CHUNK, H, t.shape[-1]), 1, 0)

    def step(state, inp):
        qi, ki, vi = inp
        scores = jnp.einsum('bihd,bjhd->bhij', qi, ki) * d_intra[None]
        intra = jnp.einsum('bhij,bjhe->bihe', scores, vi)
        cross = jnp.einsum('bihd,bhde->bihe', qi * xi[None, :, :, None], state)
        state = state * gamma_chunk[None, :, None, None] + jnp.einsum(
            'bjhd,bjhe->bhde', ki * zeta[None, :, :, None], vi)
        return state, intra + cross

    state0 = jnp.zeros((B, H, Dk, Dv), jnp.float32)
    _, y = lax.scan(step, state0, (to_chunks(q), to_chunks(k), to_chunks(v)))
    y = jnp.moveaxis(y, 0, 1).reshape(B, S, H, Dv)
    y = y * lax.rsqrt(jnp.mean(y * y, axis=-1, keepdims=True) + EPS)
    y = y.reshape(B, S, RET_V_WIDTH).astype(h.dtype)
    return (jax.nn.silu(g) * y) @ w_out


def sgu_mixer(h, w_in, b_in, ln_g, ln_b, w_s, b_s, w_out):
    B, S, _ = h.shape
    nb = S // SGU_BLOCK
    z = jax.nn.gelu(h @ w_in + b_in)
    u, v = jnp.split(z, 2, axis=-1)
    v = layernorm(v, ln_g, ln_b)
    pos_chunk = jnp.arange(SGU_BLOCK) // CHUNK
    mask = (pos_chunk[None, :] <= pos_chunk[:, None]).astype(w_s.dtype)
    vb = v.reshape(B, nb, SGU_BLOCK, SGU_GROUPS, SGU_GROUP_DIM)
    s = jnp.einsum('gij,bnjgc->bnigc', w_s * mask[None], vb) + b_s.T[None, None, :, :, None]
    s = s.reshape(B, S, SGU_HALF)
    return (u * s) @ w_out


def peer_ffn(h, w_query, sub_keys, expert_u, expert_v):
    B, S, D = h.shape
    T = B * S
    Hh, K = PEER_HEADS, PEER_TOPK
    ht = h.reshape(T, D)
    q = (ht @ w_query).reshape(T, Hh, 2, PEER_HALF).astype(jnp.float32)
    scores = jnp.einsum('thpd,hpkd->thpk', q, sub_keys.astype(jnp.float32))
    s_top, i_top = lax.top_k(scores, K)
    cand = (s_top[:, :, 0, :, None] + s_top[:, :, 1, None, :]).reshape(T, Hh, K * K)
    cand_idx = (i_top[:, :, 0, :, None] * PEER_NKEYS + i_top[:, :, 1, None, :]).reshape(T, Hh, K * K)
    best, pos = lax.top_k(cand, K)
    expert_idx = jnp.take_along_axis(cand_idx, pos, axis=-1)
    gates = jax.nn.softmax(best, axis=-1).astype(h.dtype)
    nb = T // PEER_TOKEN_BLOCK

    def block(args):
        xb, idx, g = args
        u = expert_u[idx]
        a = jax.nn.gelu(jnp.einsum('tkd,td->tk', u, xb)) * g
        return jnp.einsum('tk,tkd->td', a, expert_v[idx])

    out = lax.map(block, (ht.reshape(nb, PEER_TOKEN_BLOCK, D),
                          expert_idx.reshape(nb, PEER_TOKEN_BLOCK, Hh * K),
                          gates.reshape(nb, PEER_TOKEN_BLOCK, Hh * K)))
    return out.reshape(B, S, D)


def setup_inputs(seed: int = 0) -> dict:
    key = jax.random.key(seed)
    ks = jax.random.split(key, 24)
    f32 = jnp.float32
    D = D_MODEL

    def nrm(k, shape, scale):
        return jax.random.normal(k, shape, f32) * scale

    return {
        "x": nrm(ks[0], (BATCH, SEQ, D), 1.0),
        "c": nrm(ks[1], (BATCH, D), 1.0),
        "positions": jnp.broadcast_to(jnp.arange(SEQ, dtype=jnp.int32), (BATCH, SEQ)),
        "norm_mix": 1.0 + nrm(ks[2], (DEPTH, D), 0.05),
        "norm_ffn": 1.0 + nrm(ks[3], (DEPTH, D), 0.05),
        "ada_w": nrm(ks[4], (DEPTH, D, 6 * D), 0.5 * D ** -0.5),
        "ada_b": nrm(ks[5], (DEPTH, 6 * D), 0.02),
        "ret_w_in": nrm(ks[6], (N_A_LAYERS, D, RET_IN_WIDTH), D ** -0.5),
        "ret_w_out": nrm(ks[7], (N_A_LAYERS, RET_V_WIDTH, D), RET_V_WIDTH ** -0.5),
        "sgu_w_in": nrm(ks[8], (N_B_LAYERS, D, SGU_FFN), D ** -0.5),
        "sgu_b_in": nrm(ks[9], (N_B_LAYERS, SGU_FFN), 0.02),
        "sgu_ln_g": 1.0 + nrm(ks[10], (N_B_LAYERS, SGU_HALF), 0.05),
        "sgu_ln_b": nrm(ks[11], (N_B_LAYERS, SGU_HALF), 0.02),
        "sgu_w_s": nrm(ks[12], (N_B_LAYERS, SGU_GROUPS, SGU_BLOCK, SGU_BLOCK), SGU_BLOCK ** -0.5),
        "sgu_b_s": 1.0 + nrm(ks[13], (N_B_LAYERS, SGU_GROUPS, SGU_BLOCK), 0.1),
        "sgu_w_out": nrm(ks[14], (N_B_LAYERS, SGU_HALF, D), SGU_HALF ** -0.5),
        "peer_w_query": nrm(ks[15], (DEPTH, D, PEER_HEADS * PEER_KEY_DIM), D ** -0.5),
        "peer_sub_keys": nrm(ks[16], (DEPTH, PEER_HEADS, 2, PEER_NKEYS, PEER_HALF), PEER_HALF ** -0.5),
        "peer_u": nrm(ks[17], (DEPTH, PEER_EXPERTS, D), D ** -0.5),
        "peer_v": nrm(ks[18], (DEPTH, PEER_EXPERTS, D), 0.5),
        "norm_final": 1.0 + nrm(ks[19], (D,), 0.05),
    }


def reference(x, c, positions, norm_mix, norm_ffn, ada_w, ada_b, ret_w_in, ret_w_out,
              sgu_w_in, sgu_b_in, sgu_ln_g, sgu_ln_b, sgu_w_s, sgu_b_s, sgu_w_out,
              peer_w_query, peer_sub_keys, peer_u, peer_v, norm_final):
    c_act = jax.nn.silu(c)
    for layer in range(DEPTH):
        mod = c_act @ ada_w[layer] + ada_b[layer]
        sh1, sc1, g1, sh2, sc2, g2 = jnp.split(mod, 6, axis=-1)
        h = modulate(rmsnorm(x, norm_mix[layer]), sh1, sc1)
        j = layer // N_MIXERS
        if layer % N_MIXERS == 0:
            y = retention_mixer(h, positions, ret_w_in[j], ret_w_out[j])
        else:
            y = sgu_mixer(h, sgu_w_in[j], sgu_b_in[j], sgu_ln_g[j], sgu_ln_b[j],
                          sgu_w_s[j], sgu_b_s[j], sgu_w_out[j])
        x = x + g1[:, None, :] * y
        h = modulate(rmsnorm(x, norm_ffn[layer]), sh2, sc2)
        x = x + g2[:, None, :] * peer_ffn(h, peer_w_query[layer], peer_sub_keys[layer],
                                          peer_u[layer], peer_v[layer])
    return rmsnorm(x, norm_final)
```

```python
import functools
import math

import jax
import jax.numpy as jnp
from jax import lax
from jax.experimental import pallas as pl
from jax.experimental.pallas import tpu as pltpu

F32 = jnp.float32
BF16 = jnp.bfloat16

EPS = 1e-6
CHUNK = 64
RET_HEADS = 4
ROPE_BASE = 10000.0
SGU_BLOCK = 128
SGU_GROUPS = 8
PEER_HEADS = 8
PEER_NKEYS = 128
PEER_HALF = 128
PEER_TOPK = 16

LANES = 128
SUBLANES = 8
VMEM_LIMIT_BYTES = 56 * 1024 * 1024

RET_BLOCK = 256
NEG_INF = float("-inf")


def _params(*sem):
    return pltpu.CompilerParams(dimension_semantics=sem, vmem_limit_bytes=VMEM_LIMIT_BYTES)


def _gelu(x):
    c = math.sqrt(2.0 / math.pi)
    return x * (0.5 * (1.0 + jnp.tanh(c * (x + 0.044715 * (x * x * x)))))


def _silu(x):
    return x * (1.0 / (1.0 + jnp.exp(-x)))


def _rms(x, gain):
    return x * lax.rsqrt(jnp.mean(x * x, axis=-1, keepdims=True) + EPS) * gain


def _norm_mod(x, gain, scale, shift):
    return _rms(x, gain) * (1.0 + scale) + shift


def _mods_kernel(c_ref, w_ref, b_ref, o_ref):
    o_ref[...] = jnp.dot(_silu(c_ref[...]), w_ref[...], preferred_element_type=F32,
                         precision=lax.Precision.HIGHEST) + b_ref[...]


def _mods(c, ada_w, ada_b):
    depth, d, n = ada_w.shape
    bsz = c.shape[0]
    cp = jnp.zeros((SUBLANES, d), F32).at[:bsz].set(c)
    tn = 1536
    out = pl.pallas_call(
        _mods_kernel,
        grid=(depth, n // tn),
        in_specs=[pl.BlockSpec((SUBLANES, d), lambda l, j: (0, 0)),
                  pl.BlockSpec((None, d, tn), lambda l, j: (l, 0, j)),
                  pl.BlockSpec((None, 1, tn), lambda l, j: (l, 0, j))],
        out_specs=pl.BlockSpec((None, SUBLANES, tn), lambda l, j: (l, 0, j)),
        out_shape=jax.ShapeDtypeStruct((depth, SUBLANES, n), F32),
        compiler_params=_params("parallel", "parallel"),
        name="adaln_mods",
    )(cp, ada_w, ada_b.reshape(depth, 1, n))
    return out[:, :bsz].reshape(depth, bsz, 6, 1, d)


def _inproj_kernel(x_ref, pos_ref, invf_ref, gain_ref, sc_ref, sh_ref, w_ref,
                   q_ref, k_ref, v_ref, g_ref, *, d, dk):
    h = _norm_mod(x_ref[...], gain_ref[...], sc_ref[...], sh_ref[...]).astype(BF16)
    ang = pos_ref[...].astype(F32) * invf_ref[...]
    cos = jnp.cos(ang)
    sin = jnp.sin(ang)
    half = dk // 2
    for base, o_ref, scale in ((0, q_ref, None), (d, k_ref, dk ** -0.5)):
        for hd in range(d // dk):
            y = jnp.dot(h, w_ref[:, base + hd * dk: base + (hd + 1) * dk], preferred_element_type=F32)
            x1 = y[:, :half]
            x2 = y[:, half:]
            o1 = x1 * cos - x2 * sin
            o2 = x1 * sin + x2 * cos
            if scale is not None:
                o1 = o1 * scale
                o2 = o2 * scale
            o_ref[:, hd * dk: hd * dk + half] = o1.astype(BF16)
            o_ref[:, hd * dk + half: (hd + 1) * dk] = o2.astype(BF16)
    vw = v_ref.shape[1]
    cw = 512
    for base, o_ref in ((2 * d, v_ref), (2 * d + vw, g_ref)):
        for j in range(vw // cw):
            o_ref[:, j * cw:(j + 1) * cw] = jnp.dot(
                h, w_ref[:, base + j * cw: base + (j + 1) * cw], preferred_element_type=F32).astype(BF16)


def _ret_kernel(q_ref, k_ref, v_ref, g_ref, dm_ref, xi_ref, zt_ref, o_ref, s_ref, *, dk, dv, gl):
    @pl.when(pl.program_id(1) == 0)
    def _():
        s_ref[...] = jnp.zeros_like(s_ref)

    for h in range(len(gl)):
        q = q_ref[:, h * dk:(h + 1) * dk]
        k = k_ref[:, h * dk:(h + 1) * dk]
        v = v_ref[:, h * dv:(h + 1) * dv]
        sc = lax.dot_general(q, k, (((1,), (1,)), ((), ())), preferred_element_type=F32) * dm_ref[h]
        intra = jnp.dot(sc.astype(BF16), v, preferred_element_type=F32)
        st = s_ref[h]
        qx = (q.astype(F32) * xi_ref[h]).astype(BF16)
        cross = jnp.dot(qx, st.astype(BF16), preferred_element_type=F32)
        kz = (k.astype(F32) * zt_ref[h]).astype(BF16)
        s_ref[h] = st * gl[h] + lax.dot_general(kz, v, (((0,), (0,)), ((), ())), preferred_element_type=F32)
        y = intra + cross
        yn = y * lax.rsqrt(jnp.mean(y * y, axis=-1, keepdims=True) + EPS)
        o_ref[:, h * dv:(h + 1) * dv] = (_silu(g_ref[:, h * dv:(h + 1) * dv].astype(F32)) * yn).astype(BF16)


def _resid_proj_kernel(y_ref, w_ref, x_ref, gate_ref, o_ref):
    o_ref[...] = x_ref[...] + gate_ref[...] * jnp.dot(y_ref[...], w_ref[...], preferred_element_type=F32)


def _retention_tables(dk):
    lb = RET_BLOCK
    log_gamma = jnp.log(1.0 - 2.0 ** (-5.0 - jnp.arange(RET_HEADS, dtype=F32)))[:, None, None]
    idx = jnp.arange(lb, dtype=F32)
    diff = idx[:, None] - idx[None, :]
    ci = jnp.arange(lb)[:, None] // CHUNK
    cj = jnp.arange(lb)[None, :] // CHUNK
    dist = jnp.where(ci == cj, jnp.abs(diff), diff)
    dm = jnp.where((cj <= ci)[None], jnp.exp(log_gamma * dist[None]), 0.0)
    xi = jnp.broadcast_to(jnp.exp(log_gamma * (idx[None, :, None] + 1.0)), (RET_HEADS, lb, dk))
    zt = jnp.broadcast_to(jnp.exp(log_gamma * (lb - 1.0 - idx[None, :, None])), (RET_HEADS, lb, dk))
    gl = tuple(float((1.0 - 2.0 ** (-5.0 - h)) ** lb) for h in range(RET_HEADS))
    return dm, xi, zt, gl


def _retention_layer(x2d, positions, bsz, seq, gain, shift, scale, gate, w_in, w_out):
    t, d = x2d.shape
    dk = d // RET_HEADS
    dv = 2 * dk
    vw = RET_HEADS * dv
    tm = min(512, seq)
    tpb = seq // tm
    half = dk // 2
    inv_freq = (1.0 / (ROPE_BASE ** (jnp.arange(half, dtype=F32) / half))).reshape(1, half)
    row = lambda i: (i, 0)
    per_b = lambda i: (i // tpb, 0, 0)
    const = lambda i: (0, 0)
    q, k, v, g = pl.pallas_call(
        functools.partial(_inproj_kernel, d=d, dk=dk),
        grid=(t // tm,),
        in_specs=[pl.BlockSpec((tm, d), row),
                  pl.BlockSpec((tm, 1), row),
                  pl.BlockSpec((1, half), const),
                  pl.BlockSpec((1, d), const),
                  pl.BlockSpec((None, 1, d), per_b),
                  pl.BlockSpec((None, 1, d), per_b),
                  pl.BlockSpec(w_in.shape, const)],
        out_specs=[pl.BlockSpec((tm, d), row), pl.BlockSpec((tm, d), row),
                   pl.BlockSpec((tm, vw), row), pl.BlockSpec((tm, vw), row)],
        out_shape=[jax.ShapeDtypeStruct((t, d), BF16), jax.ShapeDtypeStruct((t, d), BF16),
                   jax.ShapeDtypeStruct((t, vw), BF16), jax.ShapeDtypeStruct((t, vw), BF16)],
        compiler_params=_params("parallel"),
        name="ret_inproj_rope",
    )(x2d, positions.reshape(t, 1), inv_freq, gain.reshape(1, d), scale, shift, w_in.astype(BF16))

    dm, xi, zt, gl = _retention_tables(dk)
    lb = RET_BLOCK
    nblk = seq // lb
    blk = lambda b, i: (b * nblk + i, 0)
    full3 = lambda b, i: (0, 0, 0)
    yg = pl.pallas_call(
        functools.partial(_ret_kernel, dk=dk, dv=dv, gl=gl),
        grid=(bsz, nblk),
        in_specs=[pl.BlockSpec((lb, d), blk), pl.BlockSpec((lb, d), blk),
                  pl.BlockSpec((lb, vw), blk), pl.BlockSpec((lb, vw), blk),
                  pl.BlockSpec(dm.shape, full3), pl.BlockSpec(xi.shape, full3), pl.BlockSpec(zt.shape, full3)],
        out_specs=pl.BlockSpec((lb, vw), blk),
        out_shape=jax.ShapeDtypeStruct((t, vw), BF16),
        scratch_shapes=[pltpu.VMEM((RET_HEADS, dk, dv), F32)],
        compiler_params=_params("parallel", "arbitrary"),
        name="ret_scan",
    )(q, k, v, g, dm, xi, zt)

    return pl.pallas_call(
        _resid_proj_kernel,
        grid=(t // tm,),
        in_specs=[pl.BlockSpec((tm, vw), row), pl.BlockSpec((vw, d), const),
                  pl.BlockSpec((tm, d), row), pl.BlockSpec((None, 1, d), per_b)],
        out_specs=pl.BlockSpec((tm, d), row),
        out_shape=jax.ShapeDtypeStruct((t, d), F32),
        compiler_params=_params("parallel"),
        name="ret_outproj",
    )(yg, w_out.astype(BF16), x2d, gate)


def _sgu_kernel(x_ref, gain_ref, sc_ref, sh_ref, gate_ref, win_ref, bin_ref, lng_ref, lnb_ref,
                ws_ref, bst_ref, wout_ref, o_ref, *, half, gdim):
    x = x_ref[...]
    tm = x.shape[0]
    h = _norm_mod(x, gain_ref[...], sc_ref[...], sh_ref[...]).astype(BF16)
    u = _gelu(jnp.dot(h, win_ref[:, :half], preferred_element_type=F32) + bin_ref[:, :half])
    v = _gelu(jnp.dot(h, win_ref[:, half:], preferred_element_type=F32) + bin_ref[:, half:])
    mu = jnp.mean(v, axis=-1, keepdims=True)
    vc = v - mu
    v = (vc * lax.rsqrt(jnp.mean(vc * vc, axis=-1, keepdims=True) + EPS) * lng_ref[...] + lnb_ref[...]).astype(BF16)
    pi = lax.broadcasted_iota(jnp.int32, (SGU_BLOCK, SGU_BLOCK), 0) // CHUNK
    pj = lax.broadcasted_iota(jnp.int32, (SGU_BLOCK, SGU_BLOCK), 1) // CHUNK
    keep = pj <= pi
    rows = []
    for n in range(tm // SGU_BLOCK):
        cols = []
        for g in range(SGU_GROUPS):
            wm = jnp.where(keep, ws_ref[g], 0.0).astype(BF16)
            vb = v[n * SGU_BLOCK:(n + 1) * SGU_BLOCK, g * gdim:(g + 1) * gdim]
            cols.append(jnp.dot(wm, vb, preferred_element_type=F32) + bst_ref[:, g:g + 1])
        rows.append(jnp.concatenate(cols, axis=1))
    s = rows[0] if len(rows) == 1 else jnp.concatenate(rows, axis=0)
    y = jnp.dot((u * s).astype(BF16), wout_ref[...], preferred_element_type=F32)
    o_ref[...] = x + gate_ref[...] * y


def _sgu_layer(x2d, seq, gain, shift, scale, gate, w_in, b_in, ln_g, ln_b, w_s, b_s, w_out):
    t, d = x2d.shape
    ffn = w_in.shape[1]
    half = ffn // 2
    tm = min(256, seq)
    tpb = seq // tm
    row = lambda i: (i, 0)
    per_b = lambda i: (i // tpb, 0, 0)
    const = lambda i: (0, 0)
    return pl.pallas_call(
        functools.partial(_sgu_kernel, half=half, gdim=half // SGU_GROUPS),
        grid=(t // tm,),
        in_specs=[pl.BlockSpec((tm, d), row),
                  pl.BlockSpec((1, d), const),
                  pl.BlockSpec((None, 1, d), per_b), pl.BlockSpec((None, 1, d), per_b),
                  pl.BlockSpec((None, 1, d), per_b),
                  pl.BlockSpec((d, ffn), const), pl.BlockSpec((1, ffn), const),
                  pl.BlockSpec((1, half), const), pl.BlockSpec((1, half), const),
                  pl.BlockSpec(w_s.shape, lambda i: (0, 0, 0)),
                  pl.BlockSpec((SGU_BLOCK, SGU_GROUPS), const),
                  pl.BlockSpec((half, d), const)],
        out_specs=pl.BlockSpec((tm, d), row),
        out_shape=jax.ShapeDtypeStruct((t, d), F32),
        compiler_params=_params("parallel"),
        name="sgu_mixer",
    )(x2d, gain.reshape(1, d), scale, shift, gate, w_in.astype(BF16), b_in.reshape(1, ffn),
      ln_g.reshape(1, half), ln_b.reshape(1, half), w_s, b_s.T, w_out.astype(BF16))


def _weff_kernel(keys_ref, wq_ref, o_ref):
    o_ref[0] = lax.dot_general(keys_ref[0, 0], wq_ref[...], (((1,), (1,)), ((), ())),
                               preferred_element_type=F32, precision=lax.Precision.HIGHEST)


def _peer_score_weights(w_query, sub_keys):
    d = w_query.shape[0]
    nhp = PEER_HEADS * 2
    out = pl.pallas_call(
        _weff_kernel,
        grid=(nhp,),
        in_specs=[pl.BlockSpec((1, 1, PEER_NKEYS, PEER_HALF), lambda i: (i // 2, i % 2, 0, 0)),
                  pl.BlockSpec((d, PEER_HALF), lambda i: (0, i))],
        out_specs=pl.BlockSpec((1, PEER_NKEYS, d), lambda i: (i, 0, 0)),
        out_shape=jax.ShapeDtypeStruct((nhp, PEER_NKEYS, d), F32),
        compiler_params=_params("parallel"),
        name="peer_score_weights",
    )(sub_keys, w_query)
    out = out.reshape(PEER_HEADS, 2, PEER_NKEYS, d).transpose(1, 2, 0, 3)
    return out.reshape(2 * PEER_NKEYS * PEER_HEADS, d).astype(BF16)


def _lex_first(vx, ix, vy, iy):
    return (vx > vy) | ((vx == vy) & (ix < iy))


def _ce(x, y):
    f = _lex_first(x[0], x[1], y[0], y[1])
    return ((jnp.where(f, x[0], y[0]), jnp.where(f, x[1], y[1])),
            (jnp.where(f, y[0], x[0]), jnp.where(f, y[1], x[1])))


def _bitonic_merge(z):
    n = len(z)
    j = n // 2
    while j >= 1:
        for i in range(n):
            l = i ^ j
            if l > i:
                z[i], z[l] = _ce(z[i], z[l])
        j //= 2
    return z


def _sort16(z):
    n = len(z)
    k = 2
    while k <= n:
        j = k // 2
        while j >= 1:
            for i in range(n):
                l = i ^ j
                if l > i:
                    if (i & k) == 0:
                        z[i], z[l] = _ce(z[i], z[l])
                    else:
                        z[l], z[i] = _ce(z[i], z[l])
            j //= 2
        k *= 2
    return z


def _top16_pairs(load, nkeys):
    k = PEER_TOPK
    top = None
    for g in range(nkeys // k):
        grp = _sort16([load(g * k + i) for i in range(k)])
        if top is None:
            top = grp
        else:
            z = []
            for i in range(k):
                a, b = top[i], grp[k - 1 - i]
                f = _lex_first(a[0], a[1], b[0], b[1])
                z.append((jnp.where(f, a[0], b[0]), jnp.where(f, a[1], b[1])))
            top = _bitonic_merge(z)
    return top


def _max_merge(z):
    n = len(z)
    j = n // 2
    while j >= 1:
        for i in range(n):
            l = i ^ j
            if l > i:
                z[i], z[l] = jnp.maximum(z[i], z[l]), jnp.minimum(z[i], z[l])
        j //= 2
    return z


def _route_kernel(x_ref, gain_ref, sc_ref, sh_ref, weff_ref,
                  ht_ref, n_ref, e1_ref, r2_ref, e2_ref, s_ref, r2s_ref, e2s_ref):
    tm = x_ref.shape[0]
    k = PEER_TOPK
    nk = PEER_NKEYS
    hp = PEER_HEADS
    h2 = _norm_mod(x_ref[...], gain_ref[...], sc_ref[...], sh_ref[...])
    h2t = h2.T.astype(BF16)
    ht_ref[...] = h2t
    s_ref[...] = jnp.dot(weff_ref[...], h2t, preferred_element_type=F32)

    def group(gi, carry):
        lane = pl.ds(pl.multiple_of(gi * LANES, LANES), LANES)

        def loader(p):
            def load(key):
                r0 = (p * nk + key) * hp
                return (s_ref[r0:r0 + hp, lane], jnp.full((hp, LANES), float(key), F32))
            return load

        top1 = _top16_pairs(loader(0), nk)
        top2 = _top16_pairs(loader(1), nk)
        v1 = [t[0] for t in top1]
        i1 = [t[1] for t in top1]
        v2 = [t[0] for t in top2]
        i2 = [t[1] for t in top2]

        ln = [k // (a + 1) for a in range(k)]
        c = [[v1[a] + v2[b] for b in range(ln[a])] for a in range(k)]
        top = list(c[0])
        for a in range(1, k):
            for i in range(k - ln[a], k):
                top[i] = jnp.maximum(top[i], c[a][k - 1 - i])
            top = _max_merge(top)
        tau = top[k - 1]
        cnt_gt = []
        cnt_eq = []
        for a in range(k):
            gt = sum(jnp.where(c[a][b] > tau, 1.0, 0.0) for b in range(ln[a]))
            ge = sum(jnp.where(c[a][b] >= tau, 1.0, 0.0) for b in range(ln[a]))
            cnt_gt.append(gt)
            cnt_eq.append(ge - gt)
        rem = float(k) - sum(cnt_gt)
        n_sel = []
        for a in range(k):
            n_sel.append(cnt_gt[a] + jnp.clip(rem, 0.0, cnt_eq[a]))
            rem = rem - cnt_eq[a]
        m = c[0][0]
        z = 0.0
        for a in range(k):
            for b in range(ln[a]):
                z = z + jnp.where(n_sel[a] > float(b), jnp.exp(c[a][b] - m), 0.0)
        inv_z = 1.0 / z

        def key_body(key, carry2):
            kf = key.astype(F32)
            rows = pl.ds(pl.multiple_of(key * hp, hp), hp)
            rows2 = pl.ds(pl.multiple_of(nk * hp + key * hp, hp), hp)
            s1 = s_ref[rows, lane]
            s2 = s_ref[rows2, lane]
            nacc = jnp.zeros((hp, LANES), F32)
            racc = jnp.full((hp, LANES), float(k), F32)
            for a in range(k):
                nacc = jnp.where(i1[a] == kf, n_sel[a], nacc)
                racc = jnp.where(i2[a] == kf, float(a), racc)
            n_ref[rows, lane] = nacc
            e1_ref[rows, lane] = jnp.exp(s1 - v1[0]) * inv_z
            r2s_ref[rows, :] = racc
            e2s_ref[rows, :] = jnp.exp(s2 - v2[0])
            return carry2

        lax.fori_loop(0, nk, key_body, 0)
        for h in range(hp):
            r2_ref[h * nk:(h + 1) * nk, lane] = r2s_ref[pl.ds(h, nk, stride=hp), :]
            e2_ref[h * nk:(h + 1) * nk, lane] = e2s_ref[pl.ds(h, nk, stride=hp), :]
        return carry

    lax.fori_loop(0, tm // LANES, group, 0)


def _peer_route(x2d, seq, gain, shift, scale, weff):
    t, d = x2d.shape
    tm = min(256, seq)
    tpb = seq // tm
    nrow = PEER_NKEYS * PEER_HEADS
    row = lambda i: (i, 0)
    col = lambda i: (0, i)
    per_b = lambda i: (i // tpb, 0, 0)
    const = lambda i: (0, 0)
    tab = jax.ShapeDtypeStruct((nrow, t), F32)
    return pl.pallas_call(
        _route_kernel,
        grid=(t // tm,),
        in_specs=[pl.BlockSpec((tm, d), row), pl.BlockSpec((1, d), const),
                  pl.BlockSpec((None, 1, d), per_b), pl.BlockSpec((None, 1, d), per_b),
                  pl.BlockSpec(weff.shape, const)],
        out_specs=[pl.BlockSpec((d, tm), col)] + [pl.BlockSpec((nrow, tm), col)] * 4,
        out_shape=[jax.ShapeDtypeStruct((d, t), BF16), tab, tab, tab, tab],
        scratch_shapes=[pltpu.VMEM((2 * nrow, tm), F32), pltpu.VMEM((nrow, LANES), F32),
                        pltpu.VMEM((nrow, LANES), F32)],
        compiler_params=_params("parallel"),
        name="peer_route",
    )(x2d, gain.reshape(1, d), scale, shift, weff)


def _experts_kernel(ht_ref, u_ref, vt_ref, r2_ref, e2_ref, n_ref, e1_ref, x_ref, gate_ref, nf_ref,
                    o_ref, acc_ref, *, final):
    j = pl.program_id(1)
    te = u_ref.shape[0]
    nk = PEER_NKEYS
    hp = PEER_HEADS

    @pl.when(j == 0)
    def _():
        acc_ref[...] = jnp.zeros_like(acc_ref)

    hid = jnp.dot(u_ref[...], ht_ref[...], preferred_element_type=F32)
    parts = []
    for s in range(te // nk):
        gacc = None
        for h in range(hp):
            r = s * hp + h
            sel = jnp.where(r2_ref[h * nk:(h + 1) * nk, :] < n_ref[r:r + 1, :],
                            e2_ref[h * nk:(h + 1) * nk, :], 0.0) * e1_ref[r:r + 1, :]
            gacc = sel if gacc is None else gacc + sel
        parts.append((_gelu(hid[s * nk:(s + 1) * nk, :]) * gacc).astype(BF16))
    act = parts[0] if len(parts) == 1 else jnp.concatenate(parts, axis=0)
    acc_ref[...] += jnp.dot(vt_ref[...], act, preferred_element_type=F32)

    @pl.when(j == pl.num_programs(1) - 1)
    def _():
        y = x_ref[...] + gate_ref[...] * acc_ref[...].T
        if final:
            y = _rms(y, nf_ref[...])
        o_ref[...] = y


def _peer_experts(x2d, seq, gate, ht, n_tab, e1_tab, r2_tab, e2_tab, u, vt, norm_final, final):
    t, d = x2d.shape
    ne = u.shape[0]
    tm = min(512, seq)
    te = 512
    tpb = seq // tm
    nrow = PEER_NKEYS * PEER_HEADS
    srow = (te // PEER_NKEYS) * PEER_HEADS
    return pl.pallas_call(
        functools.partial(_experts_kernel, final=final),
        grid=(t // tm, ne // te),
        in_specs=[pl.BlockSpec((d, tm), lambda i, j: (0, i)),
                  pl.BlockSpec((te, d), lambda i, j: (j, 0)),
                  pl.BlockSpec((d, te), lambda i, j: (0, j)),
                  pl.BlockSpec((nrow, tm), lambda i, j: (0, i)),
                  pl.BlockSpec((nrow, tm), lambda i, j: (0, i)),
                  pl.BlockSpec((srow, tm), lambda i, j: (j, i)),
                  pl.BlockSpec((srow, tm), lambda i, j: (j, i)),
                  pl.BlockSpec((tm, d), lambda i, j: (i, 0)),
                  pl.BlockSpec((None, 1, d), lambda i, j: (i // tpb, 0, 0)),
                  pl.BlockSpec((1, d), lambda i, j: (0, 0))],
        out_specs=pl.BlockSpec((tm, d), lambda i, j: (i, 0)),
        out_shape=jax.ShapeDtypeStruct((t, d), F32),
        scratch_shapes=[pltpu.VMEM((d, tm), F32)],
        compiler_params=_params("parallel", "arbitrary"),
        name="peer_experts",
    )(ht, u, vt, r2_tab, e2_tab, n_tab, e1_tab, x2d, gate, norm_final.reshape(1, d))


def _peer_layer(x2d, seq, gain, shift, scale, gate, w_query, sub_keys, u, v, norm_final, final):
    weff = _peer_score_weights(w_query, sub_keys)
    ht, n_tab, e1_tab, r2_tab, e2_tab = _peer_route(x2d, seq, gain, shift, scale, weff)
    return _peer_experts(x2d, seq, gate, ht, n_tab, e1_tab, r2_tab, e2_tab,
                         u.astype(BF16), v.T.astype(BF16), norm_final, final)


def kernel(x, c, positions, norm_mix, norm_ffn, ada_w, ada_b, ret_w_in, ret_w_out, sgu_w_in, sgu_b_in,
           sgu_ln_g, sgu_ln_b, sgu_w_s, sgu_b_s, sgu_w_out, peer_w_query, peer_sub_keys, peer_u, peer_v,
           norm_final):
    bsz, seq, d = x.shape
    depth = ada_w.shape[0]
    mods = _mods(c, ada_w, ada_b)
    xt = x.reshape(bsz * seq, d)
    for layer in range(depth):
        sh1, sc1, g1, sh2, sc2, g2 = (mods[layer, :, i] for i in range(6))
        j = layer // 2
        if layer % 2 == 0:
            xt = _retention_layer(xt, positions, bsz, seq, norm_mix[layer], sh1, sc1, g1,
                                  ret_w_in[j], ret_w_out[j])
        else:
            xt = _sgu_layer(xt, seq, norm_mix[layer], sh1, sc1, g1, sgu_w_in[j], sgu_b_in[j],
                            sgu_ln_g[j], sgu_ln_b[j], sgu_w_s[j], sgu_b_s[j], sgu_w_out[j])
        xt = _peer_layer(xt, seq, norm_ffn[layer], sh2, sc2, g2, peer_w_query[layer], peer_sub_keys[layer],
                         peer_u[layer], peer_v[layer], norm_final, layer == depth - 1)
    return xt.reshape(bsz, seq, d)
```

```python
import functools
import math

import jax
import jax.numpy as jnp
from jax import lax
from jax.experimental import pallas as pl
from jax.experimental.pallas import tpu as pltpu

F32 = jnp.float32
BF16 = jnp.bfloat16

EPS = 1e-6
CHUNK = 64
RET_HEADS = 4
ROPE_BASE = 10000.0
SGU_BLOCK = 128
SGU_GROUPS = 8
PEER_HEADS = 8
PEER_NKEYS = 128
PEER_HALF = 128
PEER_TOPK = 16

LANES = 128
SUBLANES = 8
VMEM_LIMIT_BYTES = 56 * 1024 * 1024

RET_BLOCK = 256
NEG_INF = float("-inf")


def _params(*sem):
    return pltpu.CompilerParams(dimension_semantics=sem, vmem_limit_bytes=VMEM_LIMIT_BYTES)


def _gelu(x):
    c = math.sqrt(2.0 / math.pi)
    return x * (0.5 * (1.0 + jnp.tanh(c * (x + 0.044715 * (x * x * x)))))


def _silu(x):
    return x * (1.0 / (1.0 + jnp.exp(-x)))


def _rms(x, gain):
    return x * lax.rsqrt(jnp.mean(x * x, axis=-1, keepdims=True) + EPS) * gain


def _norm_mod(x, gain, scale, shift):
    return _rms(x, gain) * (1.0 + scale) + shift


def _mods_kernel(c_ref, w_ref, b_ref, o_ref):
    o_ref[...] = jnp.dot(_silu(c_ref[...]), w_ref[...], preferred_element_type=F32,
                         precision=lax.Precision.HIGHEST) + b_ref[...]


def _mods(c, ada_w, ada_b):
    depth, d, n = ada_w.shape
    bsz = c.shape[0]
    cp = jnp.zeros((SUBLANES, d), F32).at[:bsz].set(c)
    tn = 1536
    out = pl.pallas_call(
        _mods_kernel,
        grid=(depth, n // tn),
        in_specs=[pl.BlockSpec((SUBLANES, d), lambda l, j: (0, 0)),
                  pl.BlockSpec((None, d, tn), lambda l, j: (l, 0, j)),
                  pl.BlockSpec((None, 1, tn), lambda l, j: (l, 0, j))],
        out_specs=pl.BlockSpec((None, SUBLANES, tn), lambda l, j: (l, 0, j)),
        out_shape=jax.ShapeDtypeStruct((depth, SUBLANES, n), F32),
        compiler_params=_params("parallel", "parallel"),
        name="adaln_mods",
    )(cp, ada_w, ada_b.reshape(depth, 1, n))
    return out[:, :bsz].reshape(depth, bsz, 6, 1, d)


def _inproj_kernel(x_ref, pos_ref, invf_ref, gain_ref, sc_ref, sh_ref, w_ref,
                   q_ref, k_ref, v_ref, g_ref, *, d, dk):
    h = _norm_mod(x_ref[...], gain_ref[...], sc_ref[...], sh_ref[...]).astype(BF16)
    ang = pos_ref[...].astype(F32) * invf_ref[...]
    cos = jnp.cos(ang)
    sin = jnp.sin(ang)
    half = dk // 2
    for base, o_ref, scale in ((0, q_ref, None), (d, k_ref, dk ** -0.5)):
        for hd in range(d // dk):
            y = jnp.dot(h, w_ref[:, base + hd * dk: base + (hd + 1) * dk], preferred_element_type=F32)
            x1 = y[:, :half]
            x2 = y[:, half:]
            o1 = x1 * cos - x2 * sin
            o2 = x1 * sin + x2 * cos
            if scale is not None:
                o1 = o1 * scale
                o2 = o2 * scale
            o_ref[:, hd * dk: hd * dk + half] = o1.astype(BF16)
            o_ref[:, hd * dk + half: (hd + 1) * dk] = o2.astype(BF16)
    vw = v_ref.shape[1]
    cw = 512
    for base, o_ref in ((2 * d, v_ref), (2 * d + vw, g_ref)):
        for j in range(vw // cw):
            o_ref[:, j * cw:(j + 1) * cw] = jnp.dot(
                h, w_ref[:, base + j * cw: base + (j + 1) * cw], preferred_element_type=F32).astype(BF16)


def _ret_kernel(q_ref, k_ref, v_ref, g_ref, dm_ref, xi_ref, zt_ref, o_ref, s_ref, *, dk, dv, gl):
    @pl.when(pl.program_id(1) == 0)
    def _():
        s_ref[...] = jnp.zeros_like(s_ref)

    for h in range(len(gl)):
        q = q_ref[:, h * dk:(h + 1) * dk]
        k = k_ref[:, h * dk:(h + 1) * dk]
        v = v_ref[:, h * dv:(h + 1) * dv]
        sc = lax.dot_general(q, k, (((1,), (1,)), ((), ())), preferred_element_type=F32) * dm_ref[h]
        intra = jnp.dot(sc.astype(BF16), v, preferred_element_type=F32)
        st = s_ref[h]
        qx = (q.astype(F32) * xi_ref[h]).astype(BF16)
        cross = jnp.dot(qx, st.astype(BF16), preferred_element_type=F32)
        kz = (k.astype(F32) * zt_ref[h]).astype(BF16)
        s_ref[h] = st * gl[h] + lax.dot_general(kz, v, (((0,), (0,)), ((), ())), preferred_element_type=F32)
        y = intra + cross
        yn = y * lax.rsqrt(jnp.mean(y * y, axis=-1, keepdims=True) + EPS)
        o_ref[:, h * dv:(h + 1) * dv] = (_silu(g_ref[:, h * dv:(h + 1) * dv].astype(F32)) * yn).astype(BF16)


def _resid_proj_kernel(y_ref, w_ref, x_ref, gate_ref, o_ref):
    o_ref[...] = x_ref[...] + gate_ref[...] * jnp.dot(y_ref[...], w_ref[...], preferred_element_type=F32)


def _retention_tables(dk):
    lb = RET_BLOCK
    log_gamma = jnp.log(1.0 - 2.0 ** (-5.0 - jnp.arange(RET_HEADS, dtype=F32)))[:, None, None]
    idx = jnp.arange(lb, dtype=F32)
    diff = idx[:, None] - idx[None, :]
    ci = jnp.arange(lb)[:, None] // CHUNK
    cj = jnp.arange(lb)[None, :] // CHUNK
    dist = jnp.where(ci == cj, jnp.abs(diff), diff)
    dm = jnp.where((cj <= ci)[None], jnp.exp(log_gamma * dist[None]), 0.0)
    xi = jnp.broadcast_to(jnp.exp(log_gamma * (idx[None, :, None] + 1.0)), (RET_HEADS, lb, dk))
    zt = jnp.broadcast_to(jnp.exp(log_gamma * (lb - 1.0 - idx[None, :, None])), (RET_HEADS, lb, dk))
    gl = tuple(float((1.0 - 2.0 ** (-5.0 - h)) ** lb) for h in range(RET_HEADS))
    return dm, xi, zt, gl


def _retention_layer(x2d, positions, bsz, seq, gain, shift, scale, gate, w_in, w_out):
    t, d = x2d.shape
    dk = d // RET_HEADS
    dv = 2 * dk
    vw = RET_HEADS * dv
    tm = min(512, seq)
    tpb = seq // tm
    half = dk // 2
    inv_freq = (1.0 / (ROPE_BASE ** (jnp.arange(half, dtype=F32) / half))).reshape(1, half)
    row = lambda i: (i, 0)
    per_b = lambda i: (i // tpb, 0, 0)
    const = lambda i: (0, 0)
    q, k, v, g = pl.pallas_call(
        functools.partial(_inproj_kernel, d=d, dk=dk),
        grid=(t // tm,),
        in_specs=[pl.BlockSpec((tm, d), row),
                  pl.BlockSpec((tm, 1), row),
                  pl.BlockSpec((1, half), const),
                  pl.BlockSpec((1, d), const),
                  pl.BlockSpec((None, 1, d), per_b),
                  pl.BlockSpec((None, 1, d), per_b),
                  pl.BlockSpec(w_in.shape, const)],
        out_specs=[pl.BlockSpec((tm, d), row), pl.BlockSpec((tm, d), row),
                   pl.BlockSpec((tm, vw), row), pl.BlockSpec((tm, vw), row)],
        out_shape=[jax.ShapeDtypeStruct((t, d), BF16), jax.ShapeDtypeStruct((t, d), BF16),
                   jax.ShapeDtypeStruct((t, vw), BF16), jax.ShapeDtypeStruct((t, vw), BF16)],
        compiler_params=_params("parallel"),
        name="ret_inproj_rope",
    )(x2d, positions.reshape(t, 1), inv_freq, gain.reshape(1, d), scale, shift, w_in.astype(BF16))

    dm, xi, zt, gl = _retention_tables(dk)
    lb = RET_BLOCK
    nblk = seq // lb
    blk = lambda b, i: (b * nblk + i, 0)
    full3 = lambda b, i: (0, 0, 0)
    yg = pl.pallas_call(
        functools.partial(_ret_kernel, dk=dk, dv=dv, gl=gl),
        grid=(bsz, nblk),
        in_specs=[pl.BlockSpec((lb, d), blk), pl.BlockSpec((lb, d), blk),
                  pl.BlockSpec((lb, vw), blk), pl.BlockSpec((lb, vw), blk),
                  pl.BlockSpec(dm.shape, full3), pl.BlockSpec(xi.shape, full3), pl.BlockSpec(zt.shape, full3)],
        out_specs=pl.BlockSpec((lb, vw), blk),
        out_shape=jax.ShapeDtypeStruct((t, vw), BF16),
        scratch_shapes=[pltpu.VMEM((RET_HEADS, dk, dv), F32)],
        compiler_params=_params("parallel", "arbitrary"),
        name="ret_scan",
    )(q, k, v, g, dm, xi, zt)

    return pl.pallas_call(
        _resid_proj_kernel,
        grid=(t // tm,),
        in_specs=[pl.BlockSpec((tm, vw), row), pl.BlockSpec((vw, d), const),
                  pl.BlockSpec((tm, d), row), pl.BlockSpec((None, 1, d), per_b)],
        out_specs=pl.BlockSpec((tm, d), row),
        out_shape=jax.ShapeDtypeStruct((t, d), F32),
        compiler_params=_params("parallel"),
        name="ret_outproj",
    )(yg, w_out.astype(BF16), x2d, gate)


def _sgu_kernel(x_ref, gain_ref, sc_ref, sh_ref, gate_ref, win_ref, bin_ref, lng_ref, lnb_ref,
                ws_ref, bst_ref, wout_ref, o_ref, *, half, gdim):
    x = x_ref[...]
    tm = x.shape[0]
    h = _norm_mod(x, gain_ref[...], sc_ref[...], sh_ref[...]).astype(BF16)
    u = _gelu(jnp.dot(h, win_ref[:, :half], preferred_element_type=F32) + bin_ref[:, :half])
    v = _gelu(jnp.dot(h, win_ref[:, half:], preferred_element_type=F32) + bin_ref[:, half:])
    mu = jnp.mean(v, axis=-1, keepdims=True)
    vc = v - mu
    v = (vc * lax.rsqrt(jnp.mean(vc * vc, axis=-1, keepdims=True) + EPS) * lng_ref[...] + lnb_ref[...]).astype(BF16)
    pi = lax.broadcasted_iota(jnp.int32, (SGU_BLOCK, SGU_BLOCK), 0) // CHUNK
    pj = lax.broadcasted_iota(jnp.int32, (SGU_BLOCK, SGU_BLOCK), 1) // CHUNK
    keep = pj <= pi
    rows = []
    for n in range(tm // SGU_BLOCK):
        cols = []
        for g in range(SGU_GROUPS):
            wm = jnp.where(keep, ws_ref[g], 0.0).astype(BF16)
            vb = v[n * SGU_BLOCK:(n + 1) * SGU_BLOCK, g * gdim:(g + 1) * gdim]
            cols.append(jnp.dot(wm, vb, preferred_element_type=F32) + bst_ref[:, g:g + 1])
        rows.append(jnp.concatenate(cols, axis=1))
    s = rows[0] if len(rows) == 1 else jnp.concatenate(rows, axis=0)
    y = jnp.dot((u * s).astype(BF16), wout_ref[...], preferred_element_type=F32)
    o_ref[...] = x + gate_ref[...] * y


def _sgu_layer(x2d, seq, gain, shift, scale, gate, w_in, b_in, ln_g, ln_b, w_s, b_s, w_out):
    t, d = x2d.shape
    ffn = w_in.shape[1]
    half = ffn // 2
    tm = min(256, seq)
    tpb = seq // tm
    row = lambda i: (i, 0)
    per_b = lambda i: (i // tpb, 0, 0)
    const = lambda i: (0, 0)
    return pl.pallas_call(
        functools.partial(_sgu_kernel, half=half, gdim=half // SGU_GROUPS),
        grid=(t // tm,),
        in_specs=[pl.BlockSpec((tm, d), row),
                  pl.BlockSpec((1, d), const),
                  pl.BlockSpec((None, 1, d), per_b), pl.BlockSpec((None, 1, d), per_b),
                  pl.BlockSpec((None, 1, d), per_b),
                  pl.BlockSpec((d, ffn), const), pl.BlockSpec((1, ffn), const),
                  pl.BlockSpec((1, half), const), pl.BlockSpec((1, half), const),
                  pl.BlockSpec(w_s.shape, lambda i: (0, 0, 0)),
                  pl.BlockSpec((SGU_BLOCK, SGU_GROUPS), const),
                  pl.BlockSpec((half, d), const)],
        out_specs=pl.BlockSpec((tm, d), row),
        out_shape=jax.ShapeDtypeStruct((t, d), F32),
        compiler_params=_params("parallel"),
        name="sgu_mixer",
    )(x2d, gain.reshape(1, d), scale, shift, gate, w_in.astype(BF16), b_in.reshape(1, ffn),
      ln_g.reshape(1, half), ln_b.reshape(1, half), w_s, b_s.T, w_out.astype(BF16))


def _weff_kernel(keys_ref, wq_ref, o_ref):
    o_ref[0] = lax.dot_general(keys_ref[0, 0], wq_ref[...], (((1,), (1,)), ((), ())),
                               preferred_element_type=F32, precision=lax.Precision.HIGHEST)


def _peer_score_weights(w_query, sub_keys):
    d = w_query.shape[0]
    nhp = PEER_HEADS * 2
    out = pl.pallas_call(
        _weff_kernel,
        grid=(nhp,),
        in_specs=[pl.BlockSpec((1, 1, PEER_NKEYS, PEER_HALF), lambda i: (i // 2, i % 2, 0, 0)),
                  pl.BlockSpec((d, PEER_HALF), lambda i: (0, i))],
        out_specs=pl.BlockSpec((1, PEER_NKEYS, d), lambda i: (i, 0, 0)),
        out_shape=jax.ShapeDtypeStruct((nhp, PEER_NKEYS, d), F32),
        compiler_params=_params("parallel"),
        name="peer_score_weights",
    )(sub_keys, w_query)
    out = out.reshape(PEER_HEADS, 2, PEER_NKEYS, d).transpose(1, 2, 0, 3)
    return out.reshape(2 * PEER_NKEYS * PEER_HEADS, d).astype(BF16)


def _dup_bf16(x):
    b = pltpu.bitcast(x, jnp.uint32)
    hi = (b + jnp.uint32(0x7FFF) + ((b >> 16) & jnp.uint32(1))) >> 16
    return hi | (hi << 16)


def _lex_first(vx, ix, vy, iy):
    return (vx > vy) | ((vx == vy) & (ix < iy))


def _ce(x, y):
    f = _lex_first(x[0], x[1], y[0], y[1])
    return ((jnp.where(f, x[0], y[0]), jnp.where(f, x[1], y[1])),
            (jnp.where(f, y[0], x[0]), jnp.where(f, y[1], x[1])))


def _bitonic_merge(z):
    n = len(z)
    j = n // 2
    while j >= 1:
        for i in range(n):
            l = i ^ j
            if l > i:
                z[i], z[l] = _ce(z[i], z[l])
        j //= 2
    return z


def _sort16(z):
    n = len(z)
    k = 2
    while k <= n:
        j = k // 2
        while j >= 1:
            for i in range(n):
                l = i ^ j
                if l > i:
                    if (i & k) == 0:
                        z[i], z[l] = _ce(z[i], z[l])
                    else:
                        z[l], z[i] = _ce(z[i], z[l])
            j //= 2
        k *= 2
    return z


def _top16_pairs(load, nkeys):
    k = PEER_TOPK
    top = None
    for g in range(nkeys // k):
        grp = _sort16([load(g * k + i) for i in range(k)])
        if top is None:
            top = grp
        else:
            z = []
            for i in range(k):
                a, b = top[i], grp[k - 1 - i]
                f = _lex_first(a[0], a[1], b[0], b[1])
                z.append((jnp.where(f, a[0], b[0]), jnp.where(f, a[1], b[1])))
            top = _bitonic_merge(z)
    return top


def _max_merge(z):
    n = len(z)
    j = n // 2
    while j >= 1:
        for i in range(n):
            l = i ^ j
            if l > i:
                z[i], z[l] = jnp.maximum(z[i], z[l]), jnp.minimum(z[i], z[l])
        j //= 2
    return z


def _route_kernel(x_ref, gain_ref, sc_ref, sh_ref, weff_ref,
                  ht_ref, n_ref, e1_ref, r2_ref, e2_ref, s_ref, r2s_ref, e2s_ref):
    tm = x_ref.shape[0]
    k = PEER_TOPK
    nk = PEER_NKEYS
    hp = PEER_HEADS
    h2 = _norm_mod(x_ref[...], gain_ref[...], sc_ref[...], sh_ref[...])
    h2t = h2.T.astype(BF16)
    ht_ref[...] = h2t
    s_ref[...] = jnp.dot(weff_ref[...], h2t, preferred_element_type=F32)

    def group(gi, carry):
        lane = pl.ds(pl.multiple_of(gi * LANES, LANES), LANES)

        def loader(p):
            def load(key):
                r0 = (p * nk + key) * hp
                return (s_ref[r0:r0 + hp, lane], jnp.full((hp, LANES), float(key), F32))
            return load

        top1 = _top16_pairs(loader(0), nk)
        top2 = _top16_pairs(loader(1), nk)
        v1 = [t[0] for t in top1]
        i1 = [t[1] for t in top1]
        v2 = [t[0] for t in top2]
        i2 = [t[1] for t in top2]

        ln = [k // (a + 1) for a in range(k)]
        c = [[v1[a] + v2[b] for b in range(ln[a])] for a in range(k)]
        top = list(c[0])
        for a in range(1, k):
            for i in range(k - ln[a], k):
                top[i] = jnp.maximum(top[i], c[a][k - 1 - i])
            top = _max_merge(top)
        tau = top[k - 1]
        cnt_gt = []
        cnt_eq = []
        for a in range(k):
            gt = sum(jnp.where(c[a][b] > tau, 1.0, 0.0) for b in range(ln[a]))
            ge = sum(jnp.where(c[a][b] >= tau, 1.0, 0.0) for b in range(ln[a]))
            cnt_gt.append(gt)
            cnt_eq.append(ge - gt)
        rem = float(k) - sum(cnt_gt)
        n_sel = []
        for a in range(k):
            n_sel.append(cnt_gt[a] + jnp.clip(rem, 0.0, cnt_eq[a]))
            rem = rem - cnt_eq[a]
        m = c[0][0]
        z = 0.0
        for a in range(k):
            for b in range(ln[a]):
                z = z + jnp.where(n_sel[a] > float(b), jnp.exp(c[a][b] - m), 0.0)
        inv_z = 1.0 / z

        def key_body(key, carry2):
            kf = key.astype(F32)
            rows = pl.ds(pl.multiple_of(key * hp, hp), hp)
            rows2 = pl.ds(pl.multiple_of(nk * hp + key * hp, hp), hp)
            s1 = s_ref[rows, lane]
            s2 = s_ref[rows2, lane]
            nacc = jnp.zeros((hp, LANES), F32)
            racc = jnp.full((hp, LANES), float(k), F32)
            for a in range(k):
                nacc = jnp.where(i1[a] == kf, n_sel[a], nacc)
                racc = jnp.where(i2[a] == kf, float(a), racc)
            n_ref[gi, rows, :] = _dup_bf16(nacc)
            e1_ref[gi, rows, :] = _dup_bf16(jnp.exp(s1 - v1[0]) * inv_z)
            r2s_ref[rows, :] = racc
            e2s_ref[rows, :] = jnp.exp(s2 - v2[0])
            return carry2

        lax.fori_loop(0, nk, key_body, 0)
        for h in range(hp):
            r2_ref[h * nk:(h + 1) * nk, lane] = r2s_ref[pl.ds(h, nk, stride=hp), :].astype(BF16)
            e2_ref[h * nk:(h + 1) * nk, lane] = e2s_ref[pl.ds(h, nk, stride=hp), :].astype(BF16)
        return carry

    lax.fori_loop(0, tm // LANES, group, 0)


def _peer_route(x2d, seq, gain, shift, scale, weff):
    t, d = x2d.shape
    tm = min(256, seq)
    tpb = seq // tm
    nrow = PEER_NKEYS * PEER_HEADS
    row = lambda i: (i, 0)
    col = lambda i: (0, i)
    per_b = lambda i: (i // tpb, 0, 0)
    const = lambda i: (0, 0)
    dup = jax.ShapeDtypeStruct((t // LANES, nrow, LANES), jnp.uint32)
    dup_spec = pl.BlockSpec((tm // LANES, nrow, LANES), lambda i: (i, 0, 0))
    tab = jax.ShapeDtypeStruct((nrow, t), BF16)
    return pl.pallas_call(
        _route_kernel,
        grid=(t // tm,),
        in_specs=[pl.BlockSpec((tm, d), row), pl.BlockSpec((1, d), const),
                  pl.BlockSpec((None, 1, d), per_b), pl.BlockSpec((None, 1, d), per_b),
                  pl.BlockSpec(weff.shape, const)],
        out_specs=[pl.BlockSpec((d, tm), col), dup_spec, dup_spec] + [pl.BlockSpec((nrow, tm), col)] * 2,
        out_shape=[jax.ShapeDtypeStruct((d, t), BF16), dup, dup, tab, tab],
        scratch_shapes=[pltpu.VMEM((2 * nrow, tm), F32), pltpu.VMEM((nrow, LANES), F32),
                        pltpu.VMEM((nrow, LANES), F32)],
        compiler_params=_params("parallel"),
        name="peer_route",
    )(x2d, gain.reshape(1, d), scale, shift, weff)


def _gelu_packed(x):
    a1 = 2.0 * math.sqrt(2.0 / math.pi) * math.log2(math.e)
    xb = x.astype(BF16)
    z = (xb * xb * (-a1 * 0.044715) + (-a1)) * xb
    return xb * (1.0 / (1.0 + jnp.exp2(z)))


def _experts_kernel(ht_ref, u_ref, vt_ref, r2_ref, e2_ref, n_ref, e1_ref, x_ref, gate_ref, nf_ref,
                    o_ref, acc_ref, act_ref, *, final):
    j = pl.program_id(1)
    te = u_ref.shape[0]
    tm = ht_ref.shape[1]
    nk = PEER_NKEYS
    hp = PEER_HEADS
    pr = 2 * SUBLANES
    chunk = 2 * nk
    zero = jnp.zeros((), BF16)

    @pl.when(j == 0)
    def _():
        acc_ref[...] = jnp.zeros_like(acc_ref)

    def bcast_row(ref, r, lg):
        row = jnp.concatenate([ref[lg * wg + g, r:r + 1, :] for g in range(wg)], axis=1)
        return pltpu.bitcast(jnp.broadcast_to(row, (nk // 2, wg * LANES)), BF16)

    wg = 2
    for c in range(te // chunk):
        hid = jnp.dot(u_ref[c * chunk:(c + 1) * chunk, :], ht_ref[...], preferred_element_type=F32)
        for lg in range(tm // (wg * LANES)):
            lanes = slice(lg * wg * LANES, (lg + 1) * wg * LANES)
            gacc = [None, None]
            for h in range(hp):
                r2 = r2_ref[h * nk:(h + 1) * nk, lanes]
                e2 = e2_ref[h * nk:(h + 1) * nk, lanes]
                for s2 in range(2):
                    r = (2 * c + s2) * hp + h
                    sel = jnp.where(r2 < bcast_row(n_ref, r, lg), e2, zero) * bcast_row(e1_ref, r, lg)
                    gacc[s2] = sel if gacc[s2] is None else gacc[s2] + sel
            for s2 in range(2):
                act_ref[c * chunk + s2 * nk: c * chunk + (s2 + 1) * nk, lanes] = (
                    _gelu_packed(hid[s2 * nk:(s2 + 1) * nk, lanes]) * gacc[s2])
    acc_ref[...] += jnp.dot(vt_ref[...], act_ref[...], preferred_element_type=F32)

    @pl.when(j == pl.num_programs(1) - 1)
    def _():
        y = x_ref[...] + gate_ref[...] * acc_ref[...].T
        if final:
            y = _rms(y, nf_ref[...])
        o_ref[...] = y


def _peer_experts(x2d, seq, gate, ht, n_tab, e1_tab, r2_tab, e2_tab, u, vt, norm_final, final):
    t, d = x2d.shape
    ne = u.shape[0]
    tm = min(512, seq)
    te = 1024
    tpb = seq // tm
    nrow = PEER_NKEYS * PEER_HEADS
    srow = (te // PEER_NKEYS) * PEER_HEADS
    return pl.pallas_call(
        functools.partial(_experts_kernel, final=final),
        grid=(t // tm, ne // te),
        in_specs=[pl.BlockSpec((d, tm), lambda i, j: (0, i)),
                  pl.BlockSpec((te, d), lambda i, j: (j, 0)),
                  pl.BlockSpec((d, te), lambda i, j: (0, j)),
                  pl.BlockSpec((nrow, tm), lambda i, j: (0, i)),
                  pl.BlockSpec((nrow, tm), lambda i, j: (0, i)),
                  pl.BlockSpec((tm // LANES, srow, LANES), lambda i, j: (i, j, 0)),
                  pl.BlockSpec((tm // LANES, srow, LANES), lambda i, j: (i, j, 0)),
                  pl.BlockSpec((tm, d), lambda i, j: (i, 0)),
                  pl.BlockSpec((None, 1, d), lambda i, j: (i // tpb, 0, 0)),
                  pl.BlockSpec((1, d), lambda i, j: (0, 0))],
        out_specs=pl.BlockSpec((tm, d), lambda i, j: (i, 0)),
        out_shape=jax.ShapeDtypeStruct((t, d), F32),
        scratch_shapes=[pltpu.VMEM((d, tm), F32), pltpu.VMEM((te, tm), BF16)],
        compiler_params=_params("parallel", "arbitrary"),
        name="peer_experts",
    )(ht, u, vt, r2_tab, e2_tab, n_tab, e1_tab, x2d, gate, norm_final.reshape(1, d))


def _peer_layer(x2d, seq, gain, shift, scale, gate, w_query, sub_keys, u, v, norm_final, final):
    weff = _peer_score_weights(w_query, sub_keys)
    ht, n_tab, e1_tab, r2_tab, e2_tab = _peer_route(x2d, seq, gain, shift, scale, weff)
    return _peer_experts(x2d, seq, gate, ht, n_tab, e1_tab, r2_tab, e2_tab,
                         u.astype(BF16), v.T.astype(BF16), norm_final, final)


def kernel(x, c, positions, norm_mix, norm_ffn, ada_w, ada_b, ret_w_in, ret_w_out, sgu_w_in, sgu_b_in,
           sgu_ln_g, sgu_ln_b, sgu_w_s, sgu_b_s, sgu_w_out, peer_w_query, peer_sub_keys, peer_u, peer_v,
           norm_final):
    bsz, seq, d = x.shape
    depth = ada_w.shape[0]
    mods = _mods(c, ada_w, ada_b)
    xt = x.reshape(bsz * seq, d)
    for layer in range(depth):
        sh1, sc1, g1, sh2, sc2, g2 = (mods[layer, :, i] for i in range(6))
        j = layer // 2
        if layer % 2 == 0:
            xt = _retention_layer(xt, positions, bsz, seq, norm_mix[layer], sh1, sc1, g1,
                                  ret_w_in[j], ret_w_out[j])
        else:
            xt = _sgu_layer(xt, seq, norm_mix[layer], sh1, sc1, g1, sgu_w_in[j], sgu_b_in[j],
                            sgu_ln_g[j], sgu_ln_b[j], sgu_w_s[j], sgu_b_s[j], sgu_w_out[j])
        xt = _peer_layer(xt, seq, norm_ffn[layer], sh2, sc2, g2, peer_w_query[layer], peer_sub_keys[layer],
                         peer_u[layer], peer_v[layer], norm_final, layer == depth - 1)
    return xt.reshape(bsz, seq, d)
```

```python
import functools
import math

import jax
import jax.numpy as jnp
from jax import lax
from jax.experimental import pallas as pl
from jax.experimental.pallas import tpu as pltpu

F32 = jnp.float32
BF16 = jnp.bfloat16

EPS = 1e-6
CHUNK = 64
RET_HEADS = 4
ROPE_BASE = 10000.0
SGU_BLOCK = 128
SGU_GROUPS = 8
PEER_HEADS = 8
PEER_NKEYS = 128
PEER_HALF = 128
PEER_TOPK = 16

LANES = 128
SUBLANES = 8
VMEM_LIMIT_BYTES = 56 * 1024 * 1024

RET_BLOCK = 256
NEG_INF = float("-inf")


def _params(*sem):
    return pltpu.CompilerParams(dimension_semantics=sem, vmem_limit_bytes=VMEM_LIMIT_BYTES)


def _gelu(x):
    c = math.sqrt(2.0 / math.pi)
    return x * (0.5 * (1.0 + jnp.tanh(c * (x + 0.044715 * (x * x * x)))))


def _silu(x):
    return x * (1.0 / (1.0 + jnp.exp(-x)))


def _rms(x, gain):
    return x * lax.rsqrt(jnp.mean(x * x, axis=-1, keepdims=True) + EPS) * gain


def _norm_mod(x, gain, scale, shift):
    return _rms(x, gain) * (1.0 + scale) + shift


def _mods_kernel(c_ref, w_ref, b_ref, o_ref):
    o_ref[...] = jnp.dot(_silu(c_ref[...]), w_ref[...], preferred_element_type=F32,
                         precision=lax.Precision.HIGHEST) + b_ref[...]


def _mods(c, ada_w, ada_b):
    depth, d, n = ada_w.shape
    bsz = c.shape[0]
    cp = jnp.zeros((SUBLANES, d), F32).at[:bsz].set(c)
    tn = 1536
    out = pl.pallas_call(
        _mods_kernel,
        grid=(depth, n // tn),
        in_specs=[pl.BlockSpec((SUBLANES, d), lambda l, j: (0, 0)),
                  pl.BlockSpec((None, d, tn), lambda l, j: (l, 0, j)),
                  pl.BlockSpec((None, 1, tn), lambda l, j: (l, 0, j))],
        out_specs=pl.BlockSpec((None, SUBLANES, tn), lambda l, j: (l, 0, j)),
        out_shape=jax.ShapeDtypeStruct((depth, SUBLANES, n), F32),
        compiler_params=_params("parallel", "parallel"),
        name="adaln_mods",
    )(cp, ada_w, ada_b.reshape(depth, 1, n))
    return out[:, :bsz].reshape(depth, bsz, 6, 1, d)


def _inproj_kernel(x_ref, pos_ref, invf_ref, gain_ref, sc_ref, sh_ref, w_ref,
                   q_ref, k_ref, v_ref, g_ref, *, d, dk):
    h = _norm_mod(x_ref[...], gain_ref[...], sc_ref[...], sh_ref[...]).astype(BF16)
    ang = pos_ref[...].astype(F32) * invf_ref[...]
    cos = jnp.cos(ang)
    sin = jnp.sin(ang)
    half = dk // 2
    for base, o_ref, scale in ((0, q_ref, None), (d, k_ref, dk ** -0.5)):
        for hd in range(d // dk):
            y = jnp.dot(h, w_ref[:, base + hd * dk: base + (hd + 1) * dk], preferred_element_type=F32)
            x1 = y[:, :half]
            x2 = y[:, half:]
            o1 = x1 * cos - x2 * sin
            o2 = x1 * sin + x2 * cos
            if scale is not None:
                o1 = o1 * scale
                o2 = o2 * scale
            o_ref[:, hd * dk: hd * dk + half] = o1.astype(BF16)
            o_ref[:, hd * dk + half: (hd + 1) * dk] = o2.astype(BF16)
    vw = v_ref.shape[1]
    cw = 512
    for base, o_ref in ((2 * d, v_ref), (2 * d + vw, g_ref)):
        for j in range(vw // cw):
            o_ref[:, j * cw:(j + 1) * cw] = jnp.dot(
                h, w_ref[:, base + j * cw: base + (j + 1) * cw], preferred_element_type=F32).astype(BF16)


def _ret_kernel(q_ref, k_ref, v_ref, g_ref, dm_ref, xi_ref, zt_ref, o_ref, s_ref, *, dk, dv, gl):
    @pl.when(pl.program_id(1) == 0)
    def _():
        s_ref[...] = jnp.zeros_like(s_ref)

    for h in range(len(gl)):
        q = q_ref[:, h * dk:(h + 1) * dk]
        k = k_ref[:, h * dk:(h + 1) * dk]
        v = v_ref[:, h * dv:(h + 1) * dv]
        sc = lax.dot_general(q, k, (((1,), (1,)), ((), ())), preferred_element_type=F32) * dm_ref[h]
        intra = jnp.dot(sc.astype(BF16), v, preferred_element_type=F32)
        st = s_ref[h]
        qx = (q.astype(F32) * xi_ref[h]).astype(BF16)
        cross = jnp.dot(qx, st.astype(BF16), preferred_element_type=F32)
        kz = (k.astype(F32) * zt_ref[h]).astype(BF16)
        s_ref[h] = st * gl[h] + lax.dot_general(kz, v, (((0,), (0,)), ((), ())), preferred_element_type=F32)
        y = intra + cross
        yn = y * lax.rsqrt(jnp.mean(y * y, axis=-1, keepdims=True) + EPS)
        o_ref[:, h * dv:(h + 1) * dv] = (_silu(g_ref[:, h * dv:(h + 1) * dv].astype(F32)) * yn).astype(BF16)


def _resid_proj_kernel(y_ref, w_ref, x_ref, gate_ref, o_ref):
    o_ref[...] = x_ref[...] + gate_ref[...] * jnp.dot(y_ref[...], w_ref[...], preferred_element_type=F32)


def _retention_tables(dk):
    lb = RET_BLOCK
    log_gamma = jnp.log(1.0 - 2.0 ** (-5.0 - jnp.arange(RET_HEADS, dtype=F32)))[:, None, None]
    idx = jnp.arange(lb, dtype=F32)
    diff = idx[:, None] - idx[None, :]
    ci = jnp.arange(lb)[:, None] // CHUNK
    cj = jnp.arange(lb)[None, :] // CHUNK
    dist = jnp.where(ci == cj, jnp.abs(diff), diff)
    dm = jnp.where((cj <= ci)[None], jnp.exp(log_gamma * dist[None]), 0.0)
    xi = jnp.broadcast_to(jnp.exp(log_gamma * (idx[None, :, None] + 1.0)), (RET_HEADS, lb, dk))
    zt = jnp.broadcast_to(jnp.exp(log_gamma * (lb - 1.0 - idx[None, :, None])), (RET_HEADS, lb, dk))
    gl = tuple(float((1.0 - 2.0 ** (-5.0 - h)) ** lb) for h in range(RET_HEADS))
    return dm, xi, zt, gl


def _retention_layer(x2d, positions, bsz, seq, gain, shift, scale, gate, w_in, w_out):
    t, d = x2d.shape
    dk = d // RET_HEADS
    dv = 2 * dk
    vw = RET_HEADS * dv
    tm = min(512, seq)
    tpb = seq // tm
    half = dk // 2
    inv_freq = (1.0 / (ROPE_BASE ** (jnp.arange(half, dtype=F32) / half))).reshape(1, half)
    row = lambda i: (i, 0)
    per_b = lambda i: (i // tpb, 0, 0)
    const = lambda i: (0, 0)
    q, k, v, g = pl.pallas_call(
        functools.partial(_inproj_kernel, d=d, dk=dk),
        grid=(t // tm,),
        in_specs=[pl.BlockSpec((tm, d), row),
                  pl.BlockSpec((tm, 1), row),
                  pl.BlockSpec((1, half), const),
                  pl.BlockSpec((1, d), const),
                  pl.BlockSpec((None, 1, d), per_b),
                  pl.BlockSpec((None, 1, d), per_b),
                  pl.BlockSpec(w_in.shape, const)],
        out_specs=[pl.BlockSpec((tm, d), row), pl.BlockSpec((tm, d), row),
                   pl.BlockSpec((tm, vw), row), pl.BlockSpec((tm, vw), row)],
        out_shape=[jax.ShapeDtypeStruct((t, d), BF16), jax.ShapeDtypeStruct((t, d), BF16),
                   jax.ShapeDtypeStruct((t, vw), BF16), jax.ShapeDtypeStruct((t, vw), BF16)],
        compiler_params=_params("parallel"),
        name="ret_inproj_rope",
    )(x2d, positions.reshape(t, 1), inv_freq, gain.reshape(1, d), scale, shift, w_in.astype(BF16))

    dm, xi, zt, gl = _retention_tables(dk)
    lb = RET_BLOCK
    nblk = seq // lb
    blk = lambda b, i: (b * nblk + i, 0)
    full3 = lambda b, i: (0, 0, 0)
    yg = pl.pallas_call(
        functools.partial(_ret_kernel, dk=dk, dv=dv, gl=gl),
        grid=(bsz, nblk),
        in_specs=[pl.BlockSpec((lb, d), blk), pl.BlockSpec((lb, d), blk),
                  pl.BlockSpec((lb, vw), blk), pl.BlockSpec((lb, vw), blk),
                  pl.BlockSpec(dm.shape, full3), pl.BlockSpec(xi.shape, full3), pl.BlockSpec(zt.shape, full3)],
        out_specs=pl.BlockSpec((lb, vw), blk),
        out_shape=jax.ShapeDtypeStruct((t, vw), BF16),
        scratch_shapes=[pltpu.VMEM((RET_HEADS, dk, dv), F32)],
        compiler_params=_params("parallel", "arbitrary"),
        name="ret_scan",
    )(q, k, v, g, dm, xi, zt)

    return pl.pallas_call(
        _resid_proj_kernel,
        grid=(t // tm,),
        in_specs=[pl.BlockSpec((tm, vw), row), pl.BlockSpec((vw, d), const),
                  pl.BlockSpec((tm, d), row), pl.BlockSpec((None, 1, d), per_b)],
        out_specs=pl.BlockSpec((tm, d), row),
        out_shape=jax.ShapeDtypeStruct((t, d), F32),
        compiler_params=_params("parallel"),
        name="ret_outproj",
    )(yg, w_out.astype(BF16), x2d, gate)


def _sgu_kernel(x_ref, gain_ref, sc_ref, sh_ref, gate_ref, win_ref, bin_ref, lng_ref, lnb_ref,
                ws_ref, bst_ref, wout_ref, o_ref, *, half, gdim):
    x = x_ref[...]
    tm = x.shape[0]
    h = _norm_mod(x, gain_ref[...], sc_ref[...], sh_ref[...]).astype(BF16)
    u = _gelu(jnp.dot(h, win_ref[:, :half], preferred_element_type=F32) + bin_ref[:, :half])
    v = _gelu(jnp.dot(h, win_ref[:, half:], preferred_element_type=F32) + bin_ref[:, half:])
    mu = jnp.mean(v, axis=-1, keepdims=True)
    vc = v - mu
    v = (vc * lax.rsqrt(jnp.mean(vc * vc, axis=-1, keepdims=True) + EPS) * lng_ref[...] + lnb_ref[...]).astype(BF16)
    pi = lax.broadcasted_iota(jnp.int32, (SGU_BLOCK, SGU_BLOCK), 0) // CHUNK
    pj = lax.broadcasted_iota(jnp.int32, (SGU_BLOCK, SGU_BLOCK), 1) // CHUNK
    keep = pj <= pi
    rows = []
    for n in range(tm // SGU_BLOCK):
        cols = []
        for g in range(SGU_GROUPS):
            wm = jnp.where(keep, ws_ref[g], 0.0).astype(BF16)
            vb = v[n * SGU_BLOCK:(n + 1) * SGU_BLOCK, g * gdim:(g + 1) * gdim]
            cols.append(jnp.dot(wm, vb, preferred_element_type=F32) + bst_ref[:, g:g + 1])
        rows.append(jnp.concatenate(cols, axis=1))
    s = rows[0] if len(rows) == 1 else jnp.concatenate(rows, axis=0)
    y = jnp.dot((u * s).astype(BF16), wout_ref[...], preferred_element_type=F32)
    o_ref[...] = x + gate_ref[...] * y


def _sgu_layer(x2d, seq, gain, shift, scale, gate, w_in, b_in, ln_g, ln_b, w_s, b_s, w_out):
    t, d = x2d.shape
    ffn = w_in.shape[1]
    half = ffn // 2
    tm = min(256, seq)
    tpb = seq // tm
    row = lambda i: (i, 0)
    per_b = lambda i: (i // tpb, 0, 0)
    const = lambda i: (0, 0)
    return pl.pallas_call(
        functools.partial(_sgu_kernel, half=half, gdim=half // SGU_GROUPS),
        grid=(t // tm,),
        in_specs=[pl.BlockSpec((tm, d), row),
                  pl.BlockSpec((1, d), const),
                  pl.BlockSpec((None, 1, d), per_b), pl.BlockSpec((None, 1, d), per_b),
                  pl.BlockSpec((None, 1, d), per_b),
                  pl.BlockSpec((d, ffn), const), pl.BlockSpec((1, ffn), const),
                  pl.BlockSpec((1, half), const), pl.BlockSpec((1, half), const),
                  pl.BlockSpec(w_s.shape, lambda i: (0, 0, 0)),
                  pl.BlockSpec((SGU_BLOCK, SGU_GROUPS), const),
                  pl.BlockSpec((half, d), const)],
        out_specs=pl.BlockSpec((tm, d), row),
        out_shape=jax.ShapeDtypeStruct((t, d), F32),
        compiler_params=_params("parallel"),
        name="sgu_mixer",
    )(x2d, gain.reshape(1, d), scale, shift, gate, w_in.astype(BF16), b_in.reshape(1, ffn),
      ln_g.reshape(1, half), ln_b.reshape(1, half), w_s, b_s.T, w_out.astype(BF16))


def _weff_kernel(keys_ref, wq_ref, o_ref):
    o_ref[0] = lax.dot_general(keys_ref[0, 0], wq_ref[...], (((1,), (1,)), ((), ())),
                               preferred_element_type=F32, precision=lax.Precision.HIGHEST)


def _peer_score_weights(w_query, sub_keys):
    d = w_query.shape[0]
    nhp = PEER_HEADS * 2
    out = pl.pallas_call(
        _weff_kernel,
        grid=(nhp,),
        in_specs=[pl.BlockSpec((1, 1, PEER_NKEYS, PEER_HALF), lambda i: (i // 2, i % 2, 0, 0)),
                  pl.BlockSpec((d, PEER_HALF), lambda i: (0, i))],
        out_specs=pl.BlockSpec((1, PEER_NKEYS, d), lambda i: (i, 0, 0)),
        out_shape=jax.ShapeDtypeStruct((nhp, PEER_NKEYS, d), F32),
        compiler_params=_params("parallel"),
        name="peer_score_weights",
    )(sub_keys, w_query)
    out = out.reshape(PEER_HEADS, 2, PEER_NKEYS, d).transpose(1, 2, 0, 3)
    return out.reshape(2 * PEER_NKEYS * PEER_HEADS, d).astype(BF16)


def _dup_bf16(x):
    b = pltpu.bitcast(x, jnp.uint32)
    hi = (b + jnp.uint32(0x7FFF) + ((b >> 16) & jnp.uint32(1))) >> 16
    return hi | (hi << 16)


def _lex_first(vx, ix, vy, iy):
    return (vx > vy) | ((vx == vy) & (ix < iy))


def _ce(x, y):
    f = _lex_first(x[0], x[1], y[0], y[1])
    return ((jnp.where(f, x[0], y[0]), jnp.where(f, x[1], y[1])),
            (jnp.where(f, y[0], x[0]), jnp.where(f, y[1], x[1])))


def _bitonic_merge(z):
    n = len(z)
    j = n // 2
    while j >= 1:
        for i in range(n):
            l = i ^ j
            if l > i:
                z[i], z[l] = _ce(z[i], z[l])
        j //= 2
    return z


def _sort16(z):
    n = len(z)
    k = 2
    while k <= n:
        j = k // 2
        while j >= 1:
            for i in range(n):
                l = i ^ j
                if l > i:
                    if (i & k) == 0:
                        z[i], z[l] = _ce(z[i], z[l])
                    else:
                        z[l], z[i] = _ce(z[i], z[l])
            j //= 2
        k *= 2
    return z


def _top16_pairs(load, nkeys):
    k = PEER_TOPK
    top = None
    for g in range(nkeys // k):
        grp = _sort16([load(g * k + i) for i in range(k)])
        if top is None:
            top = grp
        else:
            z = []
            for i in range(k):
                a, b = top[i], grp[k - 1 - i]
                f = _lex_first(a[0], a[1], b[0], b[1])
                z.append((jnp.where(f, a[0], b[0]), jnp.where(f, a[1], b[1])))
            top = _bitonic_merge(z)
    return top


def _max_merge(z):
    n = len(z)
    j = n // 2
    while j >= 1:
        for i in range(n):
            l = i ^ j
            if l > i:
                z[i], z[l] = jnp.maximum(z[i], z[l]), jnp.minimum(z[i], z[l])
        j //= 2
    return z


def _sort16_vals(z):
    n = len(z)
    k = 2
    while k <= n:
        j = k // 2
        while j >= 1:
            for i in range(n):
                l = i ^ j
                if l > i:
                    hi, lo = jnp.maximum(z[i], z[l]), jnp.minimum(z[i], z[l])
                    z[i], z[l] = (hi, lo) if (i & k) == 0 else (lo, hi)
            j //= 2
        k *= 2
    return z


def _top16_vals(load, nkeys):
    k = PEER_TOPK
    top = None
    for g in range(nkeys // k):
        grp = _sort16_vals([load(g * k + i) for i in range(k)])
        top = grp if top is None else _max_merge([jnp.maximum(top[i], grp[k - 1 - i]) for i in range(k)])
    return top


def _route_kernel(x_ref, gain_ref, sc_ref, sh_ref, weff_ref,
                  ht_ref, n_ref, e1_ref, r2_ref, e2_ref, s_ref, r2s_ref, e2s_ref, m_ref, mode_ref):
    tm = x_ref.shape[0]
    k = PEER_TOPK
    nk = PEER_NKEYS
    hp = PEER_HEADS
    h2 = _norm_mod(x_ref[...], gain_ref[...], sc_ref[...], sh_ref[...])
    h2t = h2.T.astype(BF16)
    ht_ref[...] = h2t
    s_ref[...] = jnp.dot(weff_ref[...], h2t, preferred_element_type=F32)

    def group(gi, carry):
        lane = pl.ds(pl.multiple_of(gi * LANES, LANES), LANES)

        def score(p, key):
            r0 = (p * nk + key) * hp
            return s_ref[r0:r0 + hp, lane]

        v = [_top16_vals(functools.partial(score, p), nk) for p in range(2)]
        ties = jnp.zeros((hp, LANES), F32)
        for p in range(2):
            for b in range(k - 1):
                ties = ties + jnp.where(v[p][b] == v[p][b + 1], 1.0, 0.0)
            ge = jnp.zeros((hp, LANES), F32)
            for key in range(nk):
                ge = ge + jnp.where(score(p, key) >= v[p][k - 1], 1.0, 0.0)
            ties = ties + jnp.where(ge > float(k), 1.0, 0.0)
        for p in range(2):
            for a in range(k):
                m_ref[(p * k + a) * hp:(p * k + a + 1) * hp, :] = v[p][a]
        mode_ref[...] = jnp.zeros((hp, LANES), F32)

        @pl.when(jnp.max(ties) > 0.0)
        def _():
            for p in range(2):
                top = _top16_pairs(
                    lambda key: (score(p, key), jnp.full((hp, LANES), float(key), F32)), nk)
                for a in range(k):
                    m_ref[(p * k + a) * hp:(p * k + a + 1) * hp, :] = top[a][1]
            mode_ref[...] = jnp.ones((hp, LANES), F32)

        v1, v2 = v

        ln = [k // (a + 1) for a in range(k)]
        c = [[v1[a] + v2[b] for b in range(ln[a])] for a in range(k)]
        top = list(c[0])
        for a in range(1, k):
            for i in range(k - ln[a], k):
                top[i] = jnp.maximum(top[i], c[a][k - 1 - i])
            top = _max_merge(top)
        tau = top[k - 1]
        cnt_gt = []
        cnt_eq = []
        for a in range(k):
            gt = sum(jnp.where(c[a][b] > tau, 1.0, 0.0) for b in range(ln[a]))
            ge = sum(jnp.where(c[a][b] >= tau, 1.0, 0.0) for b in range(ln[a]))
            cnt_gt.append(gt)
            cnt_eq.append(ge - gt)
        rem = float(k) - sum(cnt_gt)
        n_sel = []
        for a in range(k):
            n_sel.append(cnt_gt[a] + jnp.clip(rem, 0.0, cnt_eq[a]))
            rem = rem - cnt_eq[a]
        m = c[0][0]
        z = 0.0
        for a in range(k):
            for b in range(ln[a]):
                z = z + jnp.where(n_sel[a] > float(b), jnp.exp(c[a][b] - m), 0.0)
        inv_z = 1.0 / z

        by_index = mode_ref[...] > 0.5

        def first_key_body(key, carry2):
            rows = pl.ds(pl.multiple_of(key * hp, hp), hp)
            s1 = s_ref[rows, lane]
            probe = jnp.where(by_index, jnp.asarray(key).astype(F32), s1)
            nacc = jnp.zeros((hp, LANES), F32)
            for a in range(k):
                nacc = jnp.where(m_ref[a * hp:(a + 1) * hp, :] == probe, n_sel[a], nacc)
            n_ref[gi, rows, :] = _dup_bf16(nacc)
            e1_ref[gi, rows, :] = _dup_bf16(jnp.exp(s1 - v1[0]) * inv_z)
            return carry2

        def second_key_body(key, carry2):
            rows = pl.ds(pl.multiple_of(key * hp, hp), hp)
            s2 = s_ref[pl.ds(pl.multiple_of(nk * hp + key * hp, hp), hp), lane]
            probe = jnp.where(by_index, jnp.asarray(key).astype(F32), s2)
            racc = jnp.full((hp, LANES), float(k), F32)
            for b in range(k):
                racc = jnp.where(m_ref[(k + b) * hp:(k + b + 1) * hp, :] == probe, float(b), racc)
            r2s_ref[rows, :] = racc
            e2s_ref[rows, :] = jnp.exp(s2 - v2[0])
            return carry2

        lax.fori_loop(0, nk, first_key_body, 0, unroll=8)
        lax.fori_loop(0, nk, second_key_body, 0, unroll=8)
        for h in range(hp):
            r2_ref[h * nk:(h + 1) * nk, lane] = r2s_ref[pl.ds(h, nk, stride=hp), :].astype(BF16)
            e2_ref[h * nk:(h + 1) * nk, lane] = e2s_ref[pl.ds(h, nk, stride=hp), :].astype(BF16)
        return carry

    lax.fori_loop(0, tm // LANES, group, 0)


def _peer_route(x2d, seq, gain, shift, scale, weff):
    t, d = x2d.shape
    tm = min(256, seq)
    tpb = seq // tm
    nrow = PEER_NKEYS * PEER_HEADS
    row = lambda i: (i, 0)
    col = lambda i: (0, i)
    per_b = lambda i: (i // tpb, 0, 0)
    const = lambda i: (0, 0)
    dup = jax.ShapeDtypeStruct((t // LANES, nrow, LANES), jnp.uint32)
    dup_spec = pl.BlockSpec((tm // LANES, nrow, LANES), lambda i: (i, 0, 0))
    tab = jax.ShapeDtypeStruct((nrow, t), BF16)
    return pl.pallas_call(
        _route_kernel,
        grid=(t // tm,),
        in_specs=[pl.BlockSpec((tm, d), row), pl.BlockSpec((1, d), const),
                  pl.BlockSpec((None, 1, d), per_b), pl.BlockSpec((None, 1, d), per_b),
                  pl.BlockSpec(weff.shape, const)],
        out_specs=[pl.BlockSpec((d, tm), col), dup_spec, dup_spec] + [pl.BlockSpec((nrow, tm), col)] * 2,
        out_shape=[jax.ShapeDtypeStruct((d, t), BF16), dup, dup, tab, tab],
        scratch_shapes=[pltpu.VMEM((2 * nrow, tm), F32), pltpu.VMEM((nrow, LANES), F32),
                        pltpu.VMEM((nrow, LANES), F32),
                        pltpu.VMEM((2 * PEER_TOPK * PEER_HEADS, LANES), F32),
                        pltpu.VMEM((PEER_HEADS, LANES), F32)],
        compiler_params=_params("parallel"),
        name="peer_route",
    )(x2d, gain.reshape(1, d), scale, shift, weff)


def _gelu_packed(x):
    a1 = 2.0 * math.sqrt(2.0 / math.pi) * math.log2(math.e)
    xb = x.astype(BF16)
    z = (xb * xb * (-a1 * 0.044715) + (-a1)) * xb
    return xb * (1.0 / (1.0 + jnp.exp2(z)))


def _experts_kernel(ht_ref, u_ref, vt_ref, r2_ref, e2_ref, n_ref, e1_ref, x_ref, gate_ref, nf_ref,
                    o_ref, acc_ref, act_ref, *, final):
    j = pl.program_id(1)
    te = u_ref.shape[0]
    tm = ht_ref.shape[1]
    nk = PEER_NKEYS
    hp = PEER_HEADS
    chunk = 2 * nk
    wg = 2
    zero = jnp.zeros((), BF16)

    @pl.when(j == 0)
    def _():
        acc_ref[...] = jnp.zeros_like(acc_ref)

    def bcast_row(ref, r, lg):
        row = jnp.concatenate([ref[lg * wg + g, r:r + 1, :] for g in range(wg)], axis=1)
        return pltpu.bitcast(jnp.broadcast_to(row, (nk // 2, wg * LANES)), BF16)

    for c in range(te // chunk):
        hid = jnp.dot(u_ref[c * chunk:(c + 1) * chunk, :], ht_ref[...], preferred_element_type=F32)
        for lg in range(tm // (wg * LANES)):
            lanes = slice(lg * wg * LANES, (lg + 1) * wg * LANES)
            gacc = [None, None]
            for h in range(hp):
                r2 = r2_ref[h * nk:(h + 1) * nk, lanes]
                e2 = e2_ref[h * nk:(h + 1) * nk, lanes]
                for s2 in range(2):
                    r = (2 * c + s2) * hp + h
                    sel = jnp.where(r2 < bcast_row(n_ref, r, lg), e2, zero) * bcast_row(e1_ref, r, lg)
                    gacc[s2] = sel if gacc[s2] is None else gacc[s2] + sel
            for s2 in range(2):
                act_ref[c * chunk + s2 * nk: c * chunk + (s2 + 1) * nk, lanes] = (
                    _gelu_packed(hid[s2 * nk:(s2 + 1) * nk, lanes]) * gacc[s2])
    acc_ref[...] += jnp.dot(vt_ref[...], act_ref[...], preferred_element_type=F32)

    @pl.when(j == pl.num_programs(1) - 1)
    def _():
        y = x_ref[...] + gate_ref[...] * acc_ref[...].T
        if final:
            y = _rms(y, nf_ref[...])
        o_ref[...] = y


def _peer_experts(x2d, seq, gate, ht, n_tab, e1_tab, r2_tab, e2_tab, u, vt, norm_final, final):
    t, d = x2d.shape
    ne = u.shape[0]
    tm = min(512, seq)
    te = 1024
    tpb = seq // tm
    nrow = PEER_NKEYS * PEER_HEADS
    srow = (te // PEER_NKEYS) * PEER_HEADS
    return pl.pallas_call(
        functools.partial(_experts_kernel, final=final),
        grid=(t // tm, ne // te),
        in_specs=[pl.BlockSpec((d, tm), lambda i, j: (0, i)),
                  pl.BlockSpec((te, d), lambda i, j: (j, 0)),
                  pl.BlockSpec((d, te), lambda i, j: (0, j)),
                  pl.BlockSpec((nrow, tm), lambda i, j: (0, i)),
                  pl.BlockSpec((nrow, tm), lambda i, j: (0, i)),
                  pl.BlockSpec((tm // LANES, srow, LANES), lambda i, j: (i, j, 0)),
                  pl.BlockSpec((tm // LANES, srow, LANES), lambda i, j: (i, j, 0)),
                  pl.BlockSpec((tm, d), lambda i, j: (i, 0)),
                  pl.BlockSpec((None, 1, d), lambda i, j: (i // tpb, 0, 0)),
                  pl.BlockSpec((1, d), lambda i, j: (0, 0))],
        out_specs=pl.BlockSpec((tm, d), lambda i, j: (i, 0)),
        out_shape=jax.ShapeDtypeStruct((t, d), F32),
        scratch_shapes=[pltpu.VMEM((d, tm), F32), pltpu.VMEM((te, tm), BF16)],
        compiler_params=_params("parallel", "arbitrary"),
        name="peer_experts",
    )(ht, u, vt, r2_tab, e2_tab, n_tab, e1_tab, x2d, gate, norm_final.reshape(1, d))


def _peer_layer(x2d, seq, gain, shift, scale, gate, w_query, sub_keys, u, v, norm_final, final):
    weff = _peer_score_weights(w_query, sub_keys)
    ht, n_tab, e1_tab, r2_tab, e2_tab = _peer_route(x2d, seq, gain, shift, scale, weff)
    return _peer_experts(x2d, seq, gate, ht, n_tab, e1_tab, r2_tab, e2_tab,
                         u.astype(BF16), v.T.astype(BF16), norm_final, final)


def kernel(x, c, positions, norm_mix, norm_ffn, ada_w, ada_b, ret_w_in, ret_w_out, sgu_w_in, sgu_b_in,
           sgu_ln_g, sgu_ln_b, sgu_w_s, sgu_b_s, sgu_w_out, peer_w_query, peer_sub_keys, peer_u, peer_v,
           norm_final):
    bsz, seq, d = x.shape
    depth = ada_w.shape[0]
    mods = _mods(c, ada_w, ada_b)
    xt = x.reshape(bsz * seq, d)
    for layer in range(depth):
        sh1, sc1, g1, sh2, sc2, g2 = (mods[layer, :, i] for i in range(6))
        j = layer // 2
        if layer % 2 == 0:
            xt = _retention_layer(xt, positions, bsz, seq, norm_mix[layer], sh1, sc1, g1,
                                  ret_w_in[j], ret_w_out[j])
        else:
            xt = _sgu_layer(xt, seq, norm_mix[layer], sh1, sc1, g1, sgu_w_in[j], sgu_b_in[j],
                            sgu_ln_g[j], sgu_ln_b[j], sgu_w_s[j], sgu_b_s[j], sgu_w_out[j])
        xt = _peer_layer(xt, seq, norm_ffn[layer], sh2, sc2, g2, peer_w_query[layer], peer_sub_keys[layer],
                         peer_u[layer], peer_v[layer], norm_final, layer == depth - 1)
    return xt.reshape(bsz, seq, d)
```

```python
import functools
import math

import jax
import jax.numpy as jnp
from jax import lax
from jax.experimental import pallas as pl
from jax.experimental.pallas import tpu as pltpu

F32 = jnp.float32
BF16 = jnp.bfloat16

EPS = 1e-6
CHUNK = 64
RET_HEADS = 4
ROPE_BASE = 10000.0
SGU_BLOCK = 128
SGU_GROUPS = 8
PEER_HEADS = 8
PEER_NKEYS = 128
PEER_HALF = 128
PEER_TOPK = 16

LANES = 128
SUBLANES = 8
VMEM_LIMIT_BYTES = 56 * 1024 * 1024

RET_BLOCK = 256
NEG_INF = float("-inf")


def _params(*sem):
    return pltpu.CompilerParams(dimension_semantics=sem, vmem_limit_bytes=VMEM_LIMIT_BYTES)


_GELU_A1 = 2.0 * math.sqrt(2.0 / math.pi) * math.log2(math.e)


def _gelu(x):
    z = (x * x * (-_GELU_A1 * 0.044715) + (-_GELU_A1)) * x
    return x * (1.0 / (1.0 + jnp.exp2(z)))


def _silu(x):
    return x * (1.0 / (1.0 + jnp.exp(-x)))


def _rms(x, gain):
    return x * lax.rsqrt(jnp.mean(x * x, axis=-1, keepdims=True) + EPS) * gain


def _norm_mod(x, gain, scale, shift):
    return _rms(x, gain) * (1.0 + scale) + shift


def _mods_kernel(c_ref, w_ref, b_ref, o_ref):
    o_ref[...] = jnp.dot(_silu(c_ref[...]), w_ref[...], preferred_element_type=F32,
                         precision=lax.Precision.HIGHEST) + b_ref[...]


def _mods(c, ada_w, ada_b):
    depth, d, n = ada_w.shape
    bsz = c.shape[0]
    cp = jnp.zeros((SUBLANES, d), F32).at[:bsz].set(c)
    tn = 1536
    out = pl.pallas_call(
        _mods_kernel,
        grid=(depth, n // tn),
        in_specs=[pl.BlockSpec((SUBLANES, d), lambda l, j: (0, 0)),
                  pl.BlockSpec((None, d, tn), lambda l, j: (l, 0, j)),
                  pl.BlockSpec((None, 1, tn), lambda l, j: (l, 0, j))],
        out_specs=pl.BlockSpec((None, SUBLANES, tn), lambda l, j: (l, 0, j)),
        out_shape=jax.ShapeDtypeStruct((depth, SUBLANES, n), F32),
        compiler_params=_params("parallel", "parallel"),
        name="adaln_mods",
    )(cp, ada_w, ada_b.reshape(depth, 1, n))
    return out[:, :bsz].reshape(depth, bsz, 6, 1, d)


def _inproj_kernel(x_ref, pos_ref, invf_ref, gain_ref, sc_ref, sh_ref, w_ref,
                   q_ref, k_ref, v_ref, g_ref, *, d, dk):
    h = _norm_mod(x_ref[...], gain_ref[...], sc_ref[...], sh_ref[...]).astype(BF16)
    ang = pos_ref[...].astype(F32) * invf_ref[...]
    cos = jnp.cos(ang)
    sin = jnp.sin(ang)
    half = dk // 2
    for base, o_ref, scale in ((0, q_ref, None), (d, k_ref, dk ** -0.5)):
        for hd in range(d // dk):
            y = jnp.dot(h, w_ref[:, base + hd * dk: base + (hd + 1) * dk], preferred_element_type=F32)
            x1 = y[:, :half]
            x2 = y[:, half:]
            o1 = x1 * cos - x2 * sin
            o2 = x1 * sin + x2 * cos
            if scale is not None:
                o1 = o1 * scale
                o2 = o2 * scale
            o_ref[:, hd * dk: hd * dk + half] = o1.astype(BF16)
            o_ref[:, hd * dk + half: (hd + 1) * dk] = o2.astype(BF16)
    vw = v_ref.shape[1]
    cw = 512
    for base, o_ref in ((2 * d, v_ref), (2 * d + vw, g_ref)):
        for j in range(vw // cw):
            o_ref[:, j * cw:(j + 1) * cw] = jnp.dot(
                h, w_ref[:, base + j * cw: base + (j + 1) * cw], preferred_element_type=F32).astype(BF16)


def _ret_kernel(q_ref, k_ref, v_ref, g_ref, dm_ref, xi_ref, zt_ref, o_ref, s_ref, *, dk, dv, gl):
    @pl.when(pl.program_id(1) == 0)
    def _():
        s_ref[...] = jnp.zeros_like(s_ref)

    for h in range(len(gl)):
        q = q_ref[:, h * dk:(h + 1) * dk]
        k = k_ref[:, h * dk:(h + 1) * dk]
        v = v_ref[:, h * dv:(h + 1) * dv]
        sc = lax.dot_general(q, k, (((1,), (1,)), ((), ())), preferred_element_type=F32) * dm_ref[h]
        intra = jnp.dot(sc.astype(BF16), v, preferred_element_type=F32)
        st = s_ref[h]
        qx = (q.astype(F32) * xi_ref[h]).astype(BF16)
        cross = jnp.dot(qx, st.astype(BF16), preferred_element_type=F32)
        kz = (k.astype(F32) * zt_ref[h]).astype(BF16)
        s_ref[h] = st * gl[h] + lax.dot_general(kz, v, (((0,), (0,)), ((), ())), preferred_element_type=F32)
        y = intra + cross
        yn = y * lax.rsqrt(jnp.mean(y * y, axis=-1, keepdims=True) + EPS)
        o_ref[:, h * dv:(h + 1) * dv] = (_silu(g_ref[:, h * dv:(h + 1) * dv].astype(F32)) * yn).astype(BF16)


def _resid_proj_kernel(y_ref, w_ref, x_ref, gate_ref, o_ref):
    o_ref[...] = x_ref[...] + gate_ref[...] * jnp.dot(y_ref[...], w_ref[...], preferred_element_type=F32)


def _retention_tables(dk):
    lb = RET_BLOCK
    log_gamma = jnp.log(1.0 - 2.0 ** (-5.0 - jnp.arange(RET_HEADS, dtype=F32)))[:, None, None]
    idx = jnp.arange(lb, dtype=F32)
    diff = idx[:, None] - idx[None, :]
    ci = jnp.arange(lb)[:, None] // CHUNK
    cj = jnp.arange(lb)[None, :] // CHUNK
    dist = jnp.where(ci == cj, jnp.abs(diff), diff)
    dm = jnp.where((cj <= ci)[None], jnp.exp(log_gamma * dist[None]), 0.0)
    xi = jnp.broadcast_to(jnp.exp(log_gamma * (idx[None, :, None] + 1.0)), (RET_HEADS, lb, dk))
    zt = jnp.broadcast_to(jnp.exp(log_gamma * (lb - 1.0 - idx[None, :, None])), (RET_HEADS, lb, dk))
    gl = tuple(float((1.0 - 2.0 ** (-5.0 - h)) ** lb) for h in range(RET_HEADS))
    return dm, xi, zt, gl


def _retention_layer(x2d, positions, bsz, seq, gain, shift, scale, gate, w_in, w_out):
    t, d = x2d.shape
    dk = d // RET_HEADS
    dv = 2 * dk
    vw = RET_HEADS * dv
    tm = min(512, seq)
    tpb = seq // tm
    half = dk // 2
    inv_freq = (1.0 / (ROPE_BASE ** (jnp.arange(half, dtype=F32) / half))).reshape(1, half)
    row = lambda i: (i, 0)
    per_b = lambda i: (i // tpb, 0, 0)
    const = lambda i: (0, 0)
    q, k, v, g = pl.pallas_call(
        functools.partial(_inproj_kernel, d=d, dk=dk),
        grid=(t // tm,),
        in_specs=[pl.BlockSpec((tm, d), row),
                  pl.BlockSpec((tm, 1), row),
                  pl.BlockSpec((1, half), const),
                  pl.BlockSpec((1, d), const),
                  pl.BlockSpec((None, 1, d), per_b),
                  pl.BlockSpec((None, 1, d), per_b),
                  pl.BlockSpec(w_in.shape, const)],
        out_specs=[pl.BlockSpec((tm, d), row), pl.BlockSpec((tm, d), row),
                   pl.BlockSpec((tm, vw), row), pl.BlockSpec((tm, vw), row)],
        out_shape=[jax.ShapeDtypeStruct((t, d), BF16), jax.ShapeDtypeStruct((t, d), BF16),
                   jax.ShapeDtypeStruct((t, vw), BF16), jax.ShapeDtypeStruct((t, vw), BF16)],
        compiler_params=_params("parallel"),
        name="ret_inproj_rope",
    )(x2d, positions.reshape(t, 1), inv_freq, gain.reshape(1, d), scale, shift, w_in.astype(BF16))

    dm, xi, zt, gl = _retention_tables(dk)
    lb = RET_BLOCK
    nblk = seq // lb
    blk = lambda b, i: (b * nblk + i, 0)
    full3 = lambda b, i: (0, 0, 0)
    yg = pl.pallas_call(
        functools.partial(_ret_kernel, dk=dk, dv=dv, gl=gl),
        grid=(bsz, nblk),
        in_specs=[pl.BlockSpec((lb, d), blk), pl.BlockSpec((lb, d), blk),
                  pl.BlockSpec((lb, vw), blk), pl.BlockSpec((lb, vw), blk),
                  pl.BlockSpec(dm.shape, full3), pl.BlockSpec(xi.shape, full3), pl.BlockSpec(zt.shape, full3)],
        out_specs=pl.BlockSpec((lb, vw), blk),
        out_shape=jax.ShapeDtypeStruct((t, vw), BF16),
        scratch_shapes=[pltpu.VMEM((RET_HEADS, dk, dv), F32)],
        compiler_params=_params("parallel", "arbitrary"),
        name="ret_scan",
    )(q, k, v, g, dm, xi, zt)

    return pl.pallas_call(
        _resid_proj_kernel,
        grid=(t // tm,),
        in_specs=[pl.BlockSpec((tm, vw), row), pl.BlockSpec((vw, d), const),
                  pl.BlockSpec((tm, d), row), pl.BlockSpec((None, 1, d), per_b)],
        out_specs=pl.BlockSpec((tm, d), row),
        out_shape=jax.ShapeDtypeStruct((t, d), F32),
        compiler_params=_params("parallel"),
        name="ret_outproj",
    )(yg, w_out.astype(BF16), x2d, gate)


def _sgu_kernel(x_ref, gain_ref, sc_ref, sh_ref, gate_ref, win_ref, bin_ref, lng_ref, lnb_ref,
                ws_ref, bst_ref, wout_ref, o_ref, *, half, gdim):
    x = x_ref[...]
    tm = x.shape[0]
    h = _norm_mod(x, gain_ref[...], sc_ref[...], sh_ref[...]).astype(BF16)
    u = _gelu((jnp.dot(h, win_ref[:, :half], preferred_element_type=F32) + bin_ref[:, :half]).astype(BF16))
    v = _gelu(jnp.dot(h, win_ref[:, half:], preferred_element_type=F32) + bin_ref[:, half:])
    mu = jnp.mean(v, axis=-1, keepdims=True)
    vc = v - mu
    v = (vc * lax.rsqrt(jnp.mean(vc * vc, axis=-1, keepdims=True) + EPS) * lng_ref[...] + lnb_ref[...]).astype(BF16)
    pi = lax.broadcasted_iota(jnp.int32, (SGU_BLOCK, SGU_BLOCK), 0) // CHUNK
    pj = lax.broadcasted_iota(jnp.int32, (SGU_BLOCK, SGU_BLOCK), 1) // CHUNK
    keep = pj <= pi
    rows = []
    for n in range(tm // SGU_BLOCK):
        cols = []
        for g in range(SGU_GROUPS):
            wm = jnp.where(keep, ws_ref[g], 0.0).astype(BF16)
            vb = v[n * SGU_BLOCK:(n + 1) * SGU_BLOCK, g * gdim:(g + 1) * gdim]
            cols.append(jnp.dot(wm, vb, preferred_element_type=F32) + bst_ref[:, g:g + 1])
        rows.append(jnp.concatenate(cols, axis=1))
    s = rows[0] if len(rows) == 1 else jnp.concatenate(rows, axis=0)
    y = jnp.dot(u * s.astype(BF16), wout_ref[...], preferred_element_type=F32)
    o_ref[...] = x + gate_ref[...] * y


def _sgu_layer(x2d, seq, gain, shift, scale, gate, w_in, b_in, ln_g, ln_b, w_s, b_s, w_out):
    t, d = x2d.shape
    ffn = w_in.shape[1]
    half = ffn // 2
    tm = min(256, seq)
    tpb = seq // tm
    row = lambda i: (i, 0)
    per_b = lambda i: (i // tpb, 0, 0)
    const = lambda i: (0, 0)
    return pl.pallas_call(
        functools.partial(_sgu_kernel, half=half, gdim=half // SGU_GROUPS),
        grid=(t // tm,),
        in_specs=[pl.BlockSpec((tm, d), row),
                  pl.BlockSpec((1, d), const),
                  pl.BlockSpec((None, 1, d), per_b), pl.BlockSpec((None, 1, d), per_b),
                  pl.BlockSpec((None, 1, d), per_b),
                  pl.BlockSpec((d, ffn), const), pl.BlockSpec((1, ffn), const),
                  pl.BlockSpec((1, half), const), pl.BlockSpec((1, half), const),
                  pl.BlockSpec(w_s.shape, lambda i: (0, 0, 0)),
                  pl.BlockSpec((SGU_BLOCK, SGU_GROUPS), const),
                  pl.BlockSpec((half, d), const)],
        out_specs=pl.BlockSpec((tm, d), row),
        out_shape=jax.ShapeDtypeStruct((t, d), F32),
        compiler_params=_params("parallel"),
        name="sgu_mixer",
    )(x2d, gain.reshape(1, d), scale, shift, gate, w_in.astype(BF16), b_in.reshape(1, ffn),
      ln_g.reshape(1, half), ln_b.reshape(1, half), w_s, b_s.T, w_out.astype(BF16))


def _weff_kernel(keys_ref, wq_ref, o_ref):
    o_ref[0] = lax.dot_general(keys_ref[0, 0], wq_ref[...], (((1,), (1,)), ((), ())),
                               preferred_element_type=F32, precision=lax.Precision.HIGHEST)


def _peer_score_weights(w_query, sub_keys):
    d = w_query.shape[0]
    nhp = PEER_HEADS * 2
    out = pl.pallas_call(
        _weff_kernel,
        grid=(nhp,),
        in_specs=[pl.BlockSpec((1, 1, PEER_NKEYS, PEER_HALF), lambda i: (i // 2, i % 2, 0, 0)),
                  pl.BlockSpec((d, PEER_HALF), lambda i: (0, i))],
        out_specs=pl.BlockSpec((1, PEER_NKEYS, d), lambda i: (i, 0, 0)),
        out_shape=jax.ShapeDtypeStruct((nhp, PEER_NKEYS, d), F32),
        compiler_params=_params("parallel"),
        name="peer_score_weights",
    )(sub_keys, w_query)
    out = out.reshape(PEER_HEADS, 2, PEER_NKEYS, d).transpose(1, 2, 0, 3)
    return out.reshape(2 * PEER_NKEYS * PEER_HEADS, d).astype(BF16)


def _dup_bf16(x):
    b = pltpu.bitcast(x, jnp.uint32)
    hi = (b + jnp.uint32(0x7FFF) + ((b >> 16) & jnp.uint32(1))) >> 16
    return hi | (hi << 16)


def _lex_first(vx, ix, vy, iy):
    return (vx > vy) | ((vx == vy) & (ix < iy))


def _ce(x, y):
    f = _lex_first(x[0], x[1], y[0], y[1])
    return ((jnp.where(f, x[0], y[0]), jnp.where(f, x[1], y[1])),
            (jnp.where(f, y[0], x[0]), jnp.where(f, y[1], x[1])))


def _bitonic_merge(z):
    n = len(z)
    j = n // 2
    while j >= 1:
        for i in range(n):
            l = i ^ j
            if l > i:
                z[i], z[l] = _ce(z[i], z[l])
        j //= 2
    return z


def _sort16(z):
    n = len(z)
    k = 2
    while k <= n:
        j = k // 2
        while j >= 1:
            for i in range(n):
                l = i ^ j
                if l > i:
                    if (i & k) == 0:
                        z[i], z[l] = _ce(z[i], z[l])
                    else:
                        z[l], z[i] = _ce(z[i], z[l])
            j //= 2
        k *= 2
    return z


def _top16_pairs(load, nkeys):
    k = PEER_TOPK
    top = None
    for g in range(nkeys // k):
        grp = _sort16([load(g * k + i) for i in range(k)])
        if top is None:
            top = grp
        else:
            z = []
            for i in range(k):
                a, b = top[i], grp[k - 1 - i]
                f = _lex_first(a[0], a[1], b[0], b[1])
                z.append((jnp.where(f, a[0], b[0]), jnp.where(f, a[1], b[1])))
            top = _bitonic_merge(z)
    return top


def _max_merge(z):
    n = len(z)
    j = n // 2
    while j >= 1:
        for i in range(n):
            l = i ^ j
            if l > i:
                z[i], z[l] = jnp.maximum(z[i], z[l]), jnp.minimum(z[i], z[l])
        j //= 2
    return z


def _sort16_vals(z):
    n = len(z)
    k = 2
    while k <= n:
        j = k // 2
        while j >= 1:
            for i in range(n):
                l = i ^ j
                if l > i:
                    hi, lo = jnp.maximum(z[i], z[l]), jnp.minimum(z[i], z[l])
                    z[i], z[l] = (hi, lo) if (i & k) == 0 else (lo, hi)
            j //= 2
        k *= 2
    return z


def _top16_vals(load, nkeys):
    k = PEER_TOPK
    top = None
    for g in range(nkeys // k):
        grp = _sort16_vals([load(g * k + i) for i in range(k)])
        top = grp if top is None else _max_merge([jnp.maximum(top[i], grp[k - 1 - i]) for i in range(k)])
    return top


def _route_kernel(x_ref, gain_ref, sc_ref, sh_ref, weff_ref,
                  ht_ref, n_ref, e1_ref, r2_ref, e2_ref, s_ref, r2s_ref, e2s_ref, m_ref, mode_ref):
    tm = x_ref.shape[0]
    k = PEER_TOPK
    nk = PEER_NKEYS
    hp = PEER_HEADS
    h2 = _norm_mod(x_ref[...], gain_ref[...], sc_ref[...], sh_ref[...])
    h2t = h2.T.astype(BF16)
    ht_ref[...] = h2t
    s_ref[...] = jnp.dot(weff_ref[...], h2t, preferred_element_type=F32)

    def group(gi, carry):
        lane = pl.ds(pl.multiple_of(gi * LANES, LANES), LANES)

        def score(p, key):
            r0 = (p * nk + key) * hp
            return s_ref[r0:r0 + hp, lane]

        v = [_top16_vals(functools.partial(score, p), nk) for p in range(2)]
        ties = jnp.zeros((hp, LANES), F32)
        for p in range(2):
            for b in range(k - 1):
                ties = ties + jnp.where(v[p][b] == v[p][b + 1], 1.0, 0.0)
            ge = jnp.zeros((hp, LANES), F32)
            for key in range(nk):
                ge = ge + jnp.where(score(p, key) >= v[p][k - 1], 1.0, 0.0)
            ties = ties + jnp.where(ge > float(k), 1.0, 0.0)
        for p in range(2):
            for a in range(k):
                m_ref[(p * k + a) * hp:(p * k + a + 1) * hp, :] = v[p][a]
        mode_ref[...] = jnp.zeros((hp, LANES), F32)

        @pl.when(jnp.max(ties) > 0.0)
        def _():
            for p in range(2):
                top = _top16_pairs(
                    lambda key: (score(p, key), jnp.full((hp, LANES), float(key), F32)), nk)
                for a in range(k):
                    m_ref[(p * k + a) * hp:(p * k + a + 1) * hp, :] = top[a][1]
            mode_ref[...] = jnp.ones((hp, LANES), F32)

        v1, v2 = v

        ln = [k // (a + 1) for a in range(k)]
        c = [[v1[a] + v2[b] for b in range(ln[a])] for a in range(k)]
        top = list(c[0])
        for a in range(1, k):
            for i in range(k - ln[a], k):
                top[i] = jnp.maximum(top[i], c[a][k - 1 - i])
            top = _max_merge(top)
        tau = top[k - 1]
        cnt_gt = []
        cnt_eq = []
        for a in range(k):
            gt = sum(jnp.where(c[a][b] > tau, 1.0, 0.0) for b in range(ln[a]))
            ge = sum(jnp.where(c[a][b] >= tau, 1.0, 0.0) for b in range(ln[a]))
            cnt_gt.append(gt)
            cnt_eq.append(ge - gt)
        rem = float(k) - sum(cnt_gt)
        n_sel = []
        for a in range(k):
            n_sel.append(cnt_gt[a] + jnp.clip(rem, 0.0, cnt_eq[a]))
            rem = rem - cnt_eq[a]
        m = c[0][0]
        z = 0.0
        for a in range(k):
            for b in range(ln[a]):
                z = z + jnp.where(n_sel[a] > float(b), jnp.exp(c[a][b] - m), 0.0)
        inv_z = 1.0 / z

        by_index = mode_ref[...] > 0.5

        def first_key_body(key, carry2):
            rows = pl.ds(pl.multiple_of(key * hp, hp), hp)
            s1 = s_ref[rows, lane]
            probe = jnp.where(by_index, jnp.asarray(key).astype(F32), s1)
            nacc = jnp.zeros((hp, LANES), F32)
            for a in range(k):
                nacc = jnp.where(m_ref[a * hp:(a + 1) * hp, :] == probe, n_sel[a], nacc)
            n_ref[gi, rows, :] = _dup_bf16(nacc)
            e1_ref[gi, rows, :] = _dup_bf16(jnp.exp(s1 - v1[0]) * inv_z)
            return carry2

        def second_key_body(key, carry2):
            rows = pl.ds(pl.multiple_of(key * hp, hp), hp)
            s2 = s_ref[pl.ds(pl.multiple_of(nk * hp + key * hp, hp), hp), lane]
            probe = jnp.where(by_index, jnp.asarray(key).astype(F32), s2)
            racc = jnp.full((hp, LANES), float(k), F32)
            for b in range(k):
                racc = jnp.where(m_ref[(k + b) * hp:(k + b + 1) * hp, :] == probe, float(b), racc)
            r2s_ref[rows, :] = racc
            e2s_ref[rows, :] = jnp.exp(s2 - v2[0])
            return carry2

        lax.fori_loop(0, nk, first_key_body, 0, unroll=8)
        lax.fori_loop(0, nk, second_key_body, 0, unroll=8)
        for h in range(hp):
            r2_ref[h * nk:(h + 1) * nk, lane] = r2s_ref[pl.ds(h, nk, stride=hp), :].astype(BF16)
            e2_ref[h * nk:(h + 1) * nk, lane] = e2s_ref[pl.ds(h, nk, stride=hp), :].astype(BF16)
        return carry

    lax.fori_loop(0, tm // LANES, group, 0)


def _peer_route(x2d, seq, gain, shift, scale, weff):
    t, d = x2d.shape
    tm = min(256, seq)
    tpb = seq // tm
    nrow = PEER_NKEYS * PEER_HEADS
    row = lambda i: (i, 0)
    col = lambda i: (0, i)
    per_b = lambda i: (i // tpb, 0, 0)
    const = lambda i: (0, 0)
    dup = jax.ShapeDtypeStruct((t // LANES, nrow, LANES), jnp.uint32)
    dup_spec = pl.BlockSpec((tm // LANES, nrow, LANES), lambda i: (i, 0, 0))
    tab = jax.ShapeDtypeStruct((nrow, t), BF16)
    return pl.pallas_call(
        _route_kernel,
        grid=(t // tm,),
        in_specs=[pl.BlockSpec((tm, d), row), pl.BlockSpec((1, d), const),
                  pl.BlockSpec((None, 1, d), per_b), pl.BlockSpec((None, 1, d), per_b),
                  pl.BlockSpec(weff.shape, const)],
        out_specs=[pl.BlockSpec((d, tm), col), dup_spec, dup_spec] + [pl.BlockSpec((nrow, tm), col)] * 2,
        out_shape=[jax.ShapeDtypeStruct((d, t), BF16), dup, dup, tab, tab],
        scratch_shapes=[pltpu.VMEM((2 * nrow, tm), F32), pltpu.VMEM((nrow, LANES), F32),
                        pltpu.VMEM((nrow, LANES), F32),
                        pltpu.VMEM((2 * PEER_TOPK * PEER_HEADS, LANES), F32),
                        pltpu.VMEM((PEER_HEADS, LANES), F32)],
        compiler_params=_params("parallel"),
        name="peer_route",
    )(x2d, gain.reshape(1, d), scale, shift, weff)


def _experts_kernel(ht_ref, u_ref, vt_ref, r2_ref, e2_ref, n_ref, e1_ref, x_ref, gate_ref, nf_ref,
                    o_ref, acc_ref, act_ref, *, final):
    j = pl.program_id(1)
    te = u_ref.shape[0]
    tm = ht_ref.shape[1]
    nk = PEER_NKEYS
    hp = PEER_HEADS
    chunk = 2 * nk
    wg = 2
    zero = jnp.zeros((), BF16)

    @pl.when(j == 0)
    def _():
        acc_ref[...] = jnp.zeros_like(acc_ref)

    def bcast_row(ref, r, lg):
        row = jnp.concatenate([ref[lg * wg + g, r:r + 1, :] for g in range(wg)], axis=1)
        return pltpu.bitcast(jnp.broadcast_to(row, (nk // 2, wg * LANES)), BF16)

    for c in range(te // chunk):
        hid = jnp.dot(u_ref[c * chunk:(c + 1) * chunk, :], ht_ref[...], preferred_element_type=F32)
        for lg in range(tm // (wg * LANES)):
            lanes = slice(lg * wg * LANES, (lg + 1) * wg * LANES)
            gacc = [None, None]
            for h in range(hp):
                r2 = r2_ref[h * nk:(h + 1) * nk, lanes]
                e2 = e2_ref[h * nk:(h + 1) * nk, lanes]
                for s2 in range(2):
                    r = (2 * c + s2) * hp + h
                    sel = jnp.where(r2 < bcast_row(n_ref, r, lg), e2, zero) * bcast_row(e1_ref, r, lg)
                    gacc[s2] = sel if gacc[s2] is None else gacc[s2] + sel
            for s2 in range(2):
                act_ref[c * chunk + s2 * nk: c * chunk + (s2 + 1) * nk, lanes] = (
                    _gelu(hid[s2 * nk:(s2 + 1) * nk, lanes].astype(BF16)) * gacc[s2])
    acc_ref[...] += jnp.dot(vt_ref[...], act_ref[...], preferred_element_type=F32)

    @pl.when(j == pl.num_programs(1) - 1)
    def _():
        y = x_ref[...] + gate_ref[...] * acc_ref[...].T
        if final:
            y = _rms(y, nf_ref[...])
        o_ref[...] = y


def _peer_experts(x2d, seq, gate, ht, n_tab, e1_tab, r2_tab, e2_tab, u, vt, norm_final, final):
    t, d = x2d.shape
    ne = u.shape[0]
    tm = min(1024, seq)
    te = 1024
    tpb = seq // tm
    nrow = PEER_NKEYS * PEER_HEADS
    srow = (te // PEER_NKEYS) * PEER_HEADS
    return pl.pallas_call(
        functools.partial(_experts_kernel, final=final),
        grid=(t // tm, ne // te),
        in_specs=[pl.BlockSpec((d, tm), lambda i, j: (0, i)),
                  pl.BlockSpec((te, d), lambda i, j: (j, 0)),
                  pl.BlockSpec((d, te), lambda i, j: (0, j)),
                  pl.BlockSpec((nrow, tm), lambda i, j: (0, i)),
                  pl.BlockSpec((nrow, tm), lambda i, j: (0, i)),
                  pl.BlockSpec((tm // LANES, srow, LANES), lambda i, j: (i, j, 0)),
                  pl.BlockSpec((tm // LANES, srow, LANES), lambda i, j: (i, j, 0)),
                  pl.BlockSpec((tm, d), lambda i, j: (i, 0)),
                  pl.BlockSpec((None, 1, d), lambda i, j: (i // tpb, 0, 0)),
                  pl.BlockSpec((1, d), lambda i, j: (0, 0))],
        out_specs=pl.BlockSpec((tm, d), lambda i, j: (i, 0)),
        out_shape=jax.ShapeDtypeStruct((t, d), F32),
        scratch_shapes=[pltpu.VMEM((d, tm), F32), pltpu.VMEM((te, tm), BF16)],
        compiler_params=_params("parallel", "arbitrary"),
        name="peer_experts",
    )(ht, u, vt, r2_tab, e2_tab, n_tab, e1_tab, x2d, gate, norm_final.reshape(1, d))


def _peer_layer(x2d, seq, gain, shift, scale, gate, w_query, sub_keys, u, v, norm_final, final):
    weff = _peer_score_weights(w_query, sub_keys)
    ht, n_tab, e1_tab, r2_tab, e2_tab = _peer_route(x2d, seq, gain, shift, scale, weff)
    return _peer_experts(x2d, seq, gate, ht, n_tab, e1_tab, r2_tab, e2_tab,
                         u.astype(BF16), v.T.astype(BF16), norm_final, final)


def kernel(x, c, positions, norm_mix, norm_ffn, ada_w, ada_b, ret_w_in, ret_w_out, sgu_w_in, sgu_b_in,
           sgu_ln_g, sgu_ln_b, sgu_w_s, sgu_b_s, sgu_w_out, peer_w_query, peer_sub_keys, peer_u, peer_v,
           norm_final):
    bsz, seq, d = x.shape
    depth = ada_w.shape[0]
    mods = _mods(c, ada_w, ada_b)
    xt = x.reshape(bsz * seq, d)
    for layer in range(depth):
        sh1, sc1, g1, sh2, sc2, g2 = (mods[layer, :, i] for i in range(6))
        j = layer // 2
        if layer % 2 == 0:
            xt = _retention_layer(xt, positions, bsz, seq, norm_mix[layer], sh1, sc1, g1,
                                  ret_w_in[j], ret_w_out[j])
        else:
            xt = _sgu_layer(xt, seq, norm_mix[layer], sh1, sc1, g1, sgu_w_in[j], sgu_b_in[j],
                            sgu_ln_g[j], sgu_ln_b[j], sgu_w_s[j], sgu_b_s[j], sgu_w_out[j])
        xt = _peer_layer(xt, seq, norm_ffn[layer], sh2, sc2, g2, peer_w_query[layer], peer_sub_keys[layer],
                         peer_u[layer], peer_v[layer], norm_final, layer == depth - 1)
    return xt.reshape(bsz, seq, d)
```

```python
import functools
import math

import jax
import jax.numpy as jnp
from jax import lax
from jax.experimental import pallas as pl
from jax.experimental.pallas import tpu as pltpu

F32 = jnp.float32
BF16 = jnp.bfloat16

EPS = 1e-6
CHUNK = 64
RET_HEADS = 4
ROPE_BASE = 10000.0
SGU_BLOCK = 128
SGU_GROUPS = 8
PEER_HEADS = 8
PEER_NKEYS = 128
PEER_HALF = 128
PEER_TOPK = 16

LANES = 128
SUBLANES = 8
VMEM_LIMIT_BYTES = 56 * 1024 * 1024

RET_BLOCK = 256
NEG_INF = float("-inf")


def _params(*sem):
    return pltpu.CompilerParams(dimension_semantics=sem, vmem_limit_bytes=VMEM_LIMIT_BYTES)


_GELU_A1 = 2.0 * math.sqrt(2.0 / math.pi) * math.log2(math.e)


def _gelu(x):
    z = (x * x * (-_GELU_A1 * 0.044715) + (-_GELU_A1)) * x
    return x * (1.0 / (1.0 + jnp.exp2(z)))


def _silu(x):
    return x * (1.0 / (1.0 + jnp.exp(-x)))


def _rms(x, gain):
    return x * lax.rsqrt(jnp.mean(x * x, axis=-1, keepdims=True) + EPS) * gain


def _norm_mod(x, gain, scale, shift):
    return _rms(x, gain) * (1.0 + scale) + shift


def _mods_kernel(c_ref, w_ref, b_ref, o_ref):
    o_ref[...] = jnp.dot(_silu(c_ref[...]), w_ref[...], preferred_element_type=F32,
                         precision=lax.Precision.HIGHEST) + b_ref[...]


def _mods(c, ada_w, ada_b):
    depth, d, n = ada_w.shape
    bsz = c.shape[0]
    cp = jnp.zeros((SUBLANES, d), F32).at[:bsz].set(c)
    tn = 1536
    out = pl.pallas_call(
        _mods_kernel,
        grid=(depth, n // tn),
        in_specs=[pl.BlockSpec((SUBLANES, d), lambda l, j: (0, 0)),
                  pl.BlockSpec((None, d, tn), lambda l, j: (l, 0, j)),
                  pl.BlockSpec((None, 1, tn), lambda l, j: (l, 0, j))],
        out_specs=pl.BlockSpec((None, SUBLANES, tn), lambda l, j: (l, 0, j)),
        out_shape=jax.ShapeDtypeStruct((depth, SUBLANES, n), F32),
        compiler_params=_params("parallel", "parallel"),
        name="adaln_mods",
    )(cp, ada_w, ada_b.reshape(depth, 1, n))
    return out[:, :bsz].reshape(depth, bsz, 6, 1, d)


def _inproj_kernel(x_ref, pos_ref, invf_ref, gain_ref, sc_ref, sh_ref, w_ref,
                   q_ref, k_ref, v_ref, g_ref, *, d, dk):
    h = _norm_mod(x_ref[...], gain_ref[...], sc_ref[...], sh_ref[...]).astype(BF16)
    ang = pos_ref[...].astype(F32) * invf_ref[...]
    cos = jnp.cos(ang)
    sin = jnp.sin(ang)
    half = dk // 2
    for base, o_ref, scale in ((0, q_ref, None), (d, k_ref, dk ** -0.5)):
        for hd in range(d // dk):
            y = jnp.dot(h, w_ref[:, base + hd * dk: base + (hd + 1) * dk], preferred_element_type=F32)
            x1 = y[:, :half]
            x2 = y[:, half:]
            o1 = x1 * cos - x2 * sin
            o2 = x1 * sin + x2 * cos
            if scale is not None:
                o1 = o1 * scale
                o2 = o2 * scale
            o_ref[:, hd * dk: hd * dk + half] = o1.astype(BF16)
            o_ref[:, hd * dk + half: (hd + 1) * dk] = o2.astype(BF16)
    vw = v_ref.shape[1]
    cw = 512
    for base, o_ref in ((2 * d, v_ref), (2 * d + vw, g_ref)):
        for j in range(vw // cw):
            o_ref[:, j * cw:(j + 1) * cw] = jnp.dot(
                h, w_ref[:, base + j * cw: base + (j + 1) * cw], preferred_element_type=F32).astype(BF16)


def _ret_kernel(q_ref, k_ref, v_ref, g_ref, dm_ref, xi_ref, zt_ref, o_ref, s_ref, *, dk, dv, gl):
    @pl.when(pl.program_id(1) == 0)
    def _():
        s_ref[...] = jnp.zeros_like(s_ref)

    for h in range(len(gl)):
        q = q_ref[:, h * dk:(h + 1) * dk]
        k = k_ref[:, h * dk:(h + 1) * dk]
        v = v_ref[:, h * dv:(h + 1) * dv]
        sc = lax.dot_general(q, k, (((1,), (1,)), ((), ())), preferred_element_type=F32) * dm_ref[h]
        intra = jnp.dot(sc.astype(BF16), v, preferred_element_type=F32)
        st = s_ref[h]
        qx = (q.astype(F32) * xi_ref[h]).astype(BF16)
        cross = jnp.dot(qx, st.astype(BF16), preferred_element_type=F32)
        kz = (k.astype(F32) * zt_ref[h]).astype(BF16)
        s_ref[h] = st * gl[h] + lax.dot_general(kz, v, (((0,), (0,)), ((), ())), preferred_element_type=F32)
        y = intra + cross
        yn = y * lax.rsqrt(jnp.mean(y * y, axis=-1, keepdims=True) + EPS)
        o_ref[:, h * dv:(h + 1) * dv] = (_silu(g_ref[:, h * dv:(h + 1) * dv].astype(F32)) * yn).astype(BF16)


def _resid_proj_kernel(y_ref, w_ref, x_ref, gate_ref, o_ref):
    o_ref[...] = x_ref[...] + gate_ref[...] * jnp.dot(y_ref[...], w_ref[...], preferred_element_type=F32)


def _retention_tables(dk):
    lb = RET_BLOCK
    log_gamma = jnp.log(1.0 - 2.0 ** (-5.0 - jnp.arange(RET_HEADS, dtype=F32)))[:, None, None]
    idx = jnp.arange(lb, dtype=F32)
    diff = idx[:, None] - idx[None, :]
    ci = jnp.arange(lb)[:, None] // CHUNK
    cj = jnp.arange(lb)[None, :] // CHUNK
    dist = jnp.where(ci == cj, jnp.abs(diff), diff)
    dm = jnp.where((cj <= ci)[None], jnp.exp(log_gamma * dist[None]), 0.0)
    xi = jnp.broadcast_to(jnp.exp(log_gamma * (idx[None, :, None] + 1.0)), (RET_HEADS, lb, dk))
    zt = jnp.broadcast_to(jnp.exp(log_gamma * (lb - 1.0 - idx[None, :, None])), (RET_HEADS, lb, dk))
    gl = tuple(float((1.0 - 2.0 ** (-5.0 - h)) ** lb) for h in range(RET_HEADS))
    return dm, xi, zt, gl


def _retention_layer(x2d, positions, bsz, seq, gain, shift, scale, gate, w_in, w_out):
    t, d = x2d.shape
    dk = d // RET_HEADS
    dv = 2 * dk
    vw = RET_HEADS * dv
    tm = min(512, seq)
    tpb = seq // tm
    half = dk // 2
    inv_freq = (1.0 / (ROPE_BASE ** (jnp.arange(half, dtype=F32) / half))).reshape(1, half)
    row = lambda i: (i, 0)
    per_b = lambda i: (i // tpb, 0, 0)
    const = lambda i: (0, 0)
    q, k, v, g = pl.pallas_call(
        functools.partial(_inproj_kernel, d=d, dk=dk),
        grid=(t // tm,),
        in_specs=[pl.BlockSpec((tm, d), row),
                  pl.BlockSpec((tm, 1), row),
                  pl.BlockSpec((1, half), const),
                  pl.BlockSpec((1, d), const),
                  pl.BlockSpec((None, 1, d), per_b),
                  pl.BlockSpec((None, 1, d), per_b),
                  pl.BlockSpec(w_in.shape, const)],
        out_specs=[pl.BlockSpec((tm, d), row), pl.BlockSpec((tm, d), row),
                   pl.BlockSpec((tm, vw), row), pl.BlockSpec((tm, vw), row)],
        out_shape=[jax.ShapeDtypeStruct((t, d), BF16), jax.ShapeDtypeStruct((t, d), BF16),
                   jax.ShapeDtypeStruct((t, vw), BF16), jax.ShapeDtypeStruct((t, vw), BF16)],
        compiler_params=_params("parallel"),
        name="ret_inproj_rope",
    )(x2d, positions.reshape(t, 1), inv_freq, gain.reshape(1, d), scale, shift, w_in.astype(BF16))

    dm, xi, zt, gl = _retention_tables(dk)
    lb = RET_BLOCK
    nblk = seq // lb
    blk = lambda b, i: (b * nblk + i, 0)
    full3 = lambda b, i: (0, 0, 0)
    yg = pl.pallas_call(
        functools.partial(_ret_kernel, dk=dk, dv=dv, gl=gl),
        grid=(bsz, nblk),
        in_specs=[pl.BlockSpec((lb, d), blk), pl.BlockSpec((lb, d), blk),
                  pl.BlockSpec((lb, vw), blk), pl.BlockSpec((lb, vw), blk),
                  pl.BlockSpec(dm.shape, full3), pl.BlockSpec(xi.shape, full3), pl.BlockSpec(zt.shape, full3)],
        out_specs=pl.BlockSpec((lb, vw), blk),
        out_shape=jax.ShapeDtypeStruct((t, vw), BF16),
        scratch_shapes=[pltpu.VMEM((RET_HEADS, dk, dv), F32)],
        compiler_params=_params("parallel", "arbitrary"),
        name="ret_scan",
    )(q, k, v, g, dm, xi, zt)

    return pl.pallas_call(
        _resid_proj_kernel,
        grid=(t // tm,),
        in_specs=[pl.BlockSpec((tm, vw), row), pl.BlockSpec((vw, d), const),
                  pl.BlockSpec((tm, d), row), pl.BlockSpec((None, 1, d), per_b)],
        out_specs=pl.BlockSpec((tm, d), row),
        out_shape=jax.ShapeDtypeStruct((t, d), F32),
        compiler_params=_params("parallel"),
        name="ret_outproj",
    )(yg, w_out.astype(BF16), x2d, gate)


def _sgu_kernel(x_ref, gain_ref, sc_ref, sh_ref, gate_ref, win_ref, bin_ref, lng_ref, lnb_ref,
                ws_ref, bst_ref, wout_ref, o_ref, *, half, gdim, sub):
    pi = lax.broadcasted_iota(jnp.int32, (SGU_BLOCK, SGU_BLOCK), 0) // CHUNK
    pj = lax.broadcasted_iota(jnp.int32, (SGU_BLOCK, SGU_BLOCK), 1) // CHUNK
    keep = pj <= pi
    wm = [jnp.where(keep, ws_ref[g], 0.0).astype(BF16) for g in range(SGU_GROUPS)]
    for r in range(x_ref.shape[0] // sub):
        rs = slice(r * sub, (r + 1) * sub)
        x = x_ref[rs, :]
        h = _norm_mod(x, gain_ref[...], sc_ref[...], sh_ref[...]).astype(BF16)
        u = _gelu((jnp.dot(h, win_ref[:, :half], preferred_element_type=F32) + bin_ref[:, :half]).astype(BF16))
        v = _gelu(jnp.dot(h, win_ref[:, half:], preferred_element_type=F32) + bin_ref[:, half:])
        mu = jnp.mean(v, axis=-1, keepdims=True)
        vc = v - mu
        v = (vc * lax.rsqrt(jnp.mean(vc * vc, axis=-1, keepdims=True) + EPS) * lng_ref[...]
             + lnb_ref[...]).astype(BF16)
        blocks = []
        for n in range(sub // SGU_BLOCK):
            cols = []
            for g in range(SGU_GROUPS):
                vb = v[n * SGU_BLOCK:(n + 1) * SGU_BLOCK, g * gdim:(g + 1) * gdim]
                cols.append(jnp.dot(wm[g], vb, preferred_element_type=F32) + bst_ref[:, g:g + 1])
            blocks.append(jnp.concatenate(cols, axis=1))
        s = blocks[0] if len(blocks) == 1 else jnp.concatenate(blocks, axis=0)
        y = jnp.dot(u * s.astype(BF16), wout_ref[...], preferred_element_type=F32)
        o_ref[rs, :] = x + gate_ref[...] * y


def _sgu_layer(x2d, seq, gain, shift, scale, gate, w_in, b_in, ln_g, ln_b, w_s, b_s, w_out):
    t, d = x2d.shape
    ffn = w_in.shape[1]
    half = ffn // 2
    tm = min(512, seq)
    sub = min(2 * SGU_BLOCK, tm)
    tpb = seq // tm
    row = lambda i: (i, 0)
    per_b = lambda i: (i // tpb, 0, 0)
    const = lambda i: (0, 0)
    once = pl.Buffered(1)
    return pl.pallas_call(
        functools.partial(_sgu_kernel, half=half, gdim=half // SGU_GROUPS, sub=sub),
        grid=(t // tm,),
        in_specs=[pl.BlockSpec((tm, d), row),
                  pl.BlockSpec((1, d), const),
                  pl.BlockSpec((None, 1, d), per_b), pl.BlockSpec((None, 1, d), per_b),
                  pl.BlockSpec((None, 1, d), per_b),
                  pl.BlockSpec((d, ffn), const, pipeline_mode=once), pl.BlockSpec((1, ffn), const),
                  pl.BlockSpec((1, half), const), pl.BlockSpec((1, half), const),
                  pl.BlockSpec(w_s.shape, lambda i: (0, 0, 0)),
                  pl.BlockSpec((SGU_BLOCK, SGU_GROUPS), const),
                  pl.BlockSpec((half, d), const, pipeline_mode=once)],
        out_specs=pl.BlockSpec((tm, d), row),
        out_shape=jax.ShapeDtypeStruct((t, d), F32),
        compiler_params=_params("parallel"),
        name="sgu_mixer",
    )(x2d, gain.reshape(1, d), scale, shift, gate, w_in.astype(BF16), b_in.reshape(1, ffn),
      ln_g.reshape(1, half), ln_b.reshape(1, half), w_s, b_s.T, w_out.astype(BF16))


def _weff_kernel(keys_ref, wq_ref, o_ref):
    o_ref[0] = lax.dot_general(keys_ref[0, 0], wq_ref[...], (((1,), (1,)), ((), ())),
                               preferred_element_type=F32, precision=lax.Precision.HIGHEST)


def _peer_score_weights(w_query, sub_keys):
    d = w_query.shape[0]
    nhp = PEER_HEADS * 2
    out = pl.pallas_call(
        _weff_kernel,
        grid=(nhp,),
        in_specs=[pl.BlockSpec((1, 1, PEER_NKEYS, PEER_HALF), lambda i: (i // 2, i % 2, 0, 0)),
                  pl.BlockSpec((d, PEER_HALF), lambda i: (0, i))],
        out_specs=pl.BlockSpec((1, PEER_NKEYS, d), lambda i: (i, 0, 0)),
        out_shape=jax.ShapeDtypeStruct((nhp, PEER_NKEYS, d), F32),
        compiler_params=_params("parallel"),
        name="peer_score_weights",
    )(sub_keys, w_query)
    out = out.reshape(PEER_HEADS, 2, PEER_NKEYS, d).transpose(1, 2, 0, 3)
    return out.reshape(2 * PEER_NKEYS * PEER_HEADS, d).astype(BF16)


def _dup_bf16(x):
    b = pltpu.bitcast(x, jnp.uint32)
    hi = (b + jnp.uint32(0x7FFF) + ((b >> 16) & jnp.uint32(1))) >> 16
    return hi | (hi << 16)


def _lex_first(vx, ix, vy, iy):
    return (vx > vy) | ((vx == vy) & (ix < iy))


def _ce(x, y):
    f = _lex_first(x[0], x[1], y[0], y[1])
    return ((jnp.where(f, x[0], y[0]), jnp.where(f, x[1], y[1])),
            (jnp.where(f, y[0], x[0]), jnp.where(f, y[1], x[1])))


def _bitonic_merge(z):
    n = len(z)
    j = n // 2
    while j >= 1:
        for i in range(n):
            l = i ^ j
            if l > i:
                z[i], z[l] = _ce(z[i], z[l])
        j //= 2
    return z


def _sort16(z):
    n = len(z)
    k = 2
    while k <= n:
        j = k // 2
        while j >= 1:
            for i in range(n):
                l = i ^ j
                if l > i:
                    if (i & k) == 0:
                        z[i], z[l] = _ce(z[i], z[l])
                    else:
                        z[l], z[i] = _ce(z[i], z[l])
            j //= 2
        k *= 2
    return z


def _top16_pairs(load, nkeys):
    k = PEER_TOPK
    top = None
    for g in range(nkeys // k):
        grp = _sort16([load(g * k + i) for i in range(k)])
        if top is None:
            top = grp
        else:
            z = []
            for i in range(k):
                a, b = top[i], grp[k - 1 - i]
                f = _lex_first(a[0], a[1], b[0], b[1])
                z.append((jnp.where(f, a[0], b[0]), jnp.where(f, a[1], b[1])))
            top = _bitonic_merge(z)
    return top


def _max_merge(z):
    n = len(z)
    j = n // 2
    while j >= 1:
        for i in range(n):
            l = i ^ j
            if l > i:
                z[i], z[l] = jnp.maximum(z[i], z[l]), jnp.minimum(z[i], z[l])
        j //= 2
    return z


def _sort16_vals(z):
    n = len(z)
    k = 2
    while k <= n:
        j = k // 2
        while j >= 1:
            for i in range(n):
                l = i ^ j
                if l > i:
                    hi, lo = jnp.maximum(z[i], z[l]), jnp.minimum(z[i], z[l])
                    z[i], z[l] = (hi, lo) if (i & k) == 0 else (lo, hi)
            j //= 2
        k *= 2
    return z


def _top16_vals(load, nkeys):
    k = PEER_TOPK
    top = None
    for g in range(nkeys // k):
        grp = _sort16_vals([load(g * k + i) for i in range(k)])
        top = grp if top is None else _max_merge([jnp.maximum(top[i], grp[k - 1 - i]) for i in range(k)])
    return top


def _route_kernel(x_ref, gain_ref, sc_ref, sh_ref, weff_ref,
                  ht_ref, n_ref, e1_ref, r2_ref, e2_ref, s_ref, r2s_ref, e2s_ref, m_ref, mode_ref):
    tm = x_ref.shape[0]
    k = PEER_TOPK
    nk = PEER_NKEYS
    hp = PEER_HEADS
    h2 = _norm_mod(x_ref[...], gain_ref[...], sc_ref[...], sh_ref[...])
    h2t = h2.T.astype(BF16)
    ht_ref[...] = h2t
    s_ref[...] = jnp.dot(weff_ref[...], h2t, preferred_element_type=F32)

    def group(gi, carry):
        lane = pl.ds(pl.multiple_of(gi * LANES, LANES), LANES)

        def score(p, key):
            r0 = (p * nk + key) * hp
            return s_ref[r0:r0 + hp, lane]

        v = [_top16_vals(functools.partial(score, p), nk) for p in range(2)]
        ties = jnp.zeros((hp, LANES), F32)
        for p in range(2):
            for b in range(k - 1):
                ties = ties + jnp.where(v[p][b] == v[p][b + 1], 1.0, 0.0)
            ge = jnp.zeros((hp, LANES), F32)
            for key in range(nk):
                ge = ge + jnp.where(score(p, key) >= v[p][k - 1], 1.0, 0.0)
            ties = ties + jnp.where(ge > float(k), 1.0, 0.0)
        for p in range(2):
            for a in range(k):
                m_ref[(p * k + a) * hp:(p * k + a + 1) * hp, :] = v[p][a]
        mode_ref[...] = jnp.zeros((hp, LANES), F32)

        @pl.when(jnp.max(ties) > 0.0)
        def _():
            for p in range(2):
                top = _top16_pairs(
                    lambda key: (score(p, key), jnp.full((hp, LANES), float(key), F32)), nk)
                for a in range(k):
                    m_ref[(p * k + a) * hp:(p * k + a + 1) * hp, :] = top[a][1]
            mode_ref[...] = jnp.ones((hp, LANES), F32)

        v1, v2 = v

        ln = [k // (a + 1) for a in range(k)]
        c = [[v1[a] + v2[b] for b in range(ln[a])] for a in range(k)]
        top = list(c[0])
        for a in range(1, k):
            for i in range(k - ln[a], k):
                top[i] = jnp.maximum(top[i], c[a][k - 1 - i])
            top = _max_merge(top)
        tau = top[k - 1]
        cnt_gt = []
        cnt_eq = []
        for a in range(k):
            gt = sum(jnp.where(c[a][b] > tau, 1.0, 0.0) for b in range(ln[a]))
            ge = sum(jnp.where(c[a][b] >= tau, 1.0, 0.0) for b in range(ln[a]))
            cnt_gt.append(gt)
            cnt_eq.append(ge - gt)
        rem = float(k) - sum(cnt_gt)
        n_sel = []
        for a in range(k):
            n_sel.append(cnt_gt[a] + jnp.clip(rem, 0.0, cnt_eq[a]))
            rem = rem - cnt_eq[a]
        m = c[0][0]
        z = 0.0
        for a in range(k):
            for b in range(ln[a]):
                z = z + jnp.where(n_sel[a] > float(b), jnp.exp(c[a][b] - m), 0.0)
        inv_z = 1.0 / z

        by_index = mode_ref[...] > 0.5

        def first_key_body(key, carry2):
            rows = pl.ds(pl.multiple_of(key * hp, hp), hp)
            s1 = s_ref[rows, lane]
            probe = jnp.where(by_index, jnp.asarray(key).astype(F32), s1)
            nacc = jnp.zeros((hp, LANES), F32)
            for a in range(k):
                nacc = jnp.where(m_ref[a * hp:(a + 1) * hp, :] == probe, n_sel[a], nacc)
            n_ref[gi, rows, :] = _dup_bf16(nacc)
            e1_ref[gi, rows, :] = _dup_bf16(jnp.exp(s1 - v1[0]) * inv_z)
            return carry2

        def second_key_body(key, carry2):
            rows = pl.ds(pl.multiple_of(key * hp, hp), hp)
            s2 = s_ref[pl.ds(pl.multiple_of(nk * hp + key * hp, hp), hp), lane]
            probe = jnp.where(by_index, jnp.asarray(key).astype(F32), s2)
            racc = jnp.full((hp, LANES), float(k), F32)
            for b in range(k):
                racc = jnp.where(m_ref[(k + b) * hp:(k + b + 1) * hp, :] == probe, float(b), racc)
            r2s_ref[rows, :] = racc
            e2s_ref[rows, :] = jnp.exp(s2 - v2[0])
            return carry2

        lax.fori_loop(0, nk, first_key_body, 0, unroll=8)
        lax.fori_loop(0, nk, second_key_body, 0, unroll=8)
        for h in range(hp):
            r2_ref[h * nk:(h + 1) * nk, lane] = r2s_ref[pl.ds(h, nk, stride=hp), :].astype(BF16)
            e2_ref[h * nk:(h + 1) * nk, lane] = e2s_ref[pl.ds(h, nk, stride=hp), :].astype(BF16)
        return carry

    lax.fori_loop(0, tm // LANES, group, 0)


def _peer_route(x2d, seq, gain, shift, scale, weff):
    t, d = x2d.shape
    tm = min(256, seq)
    tpb = seq // tm
    nrow = PEER_NKEYS * PEER_HEADS
    row = lambda i: (i, 0)
    col = lambda i: (0, i)
    per_b = lambda i: (i // tpb, 0, 0)
    const = lambda i: (0, 0)
    dup = jax.ShapeDtypeStruct((t // LANES, nrow, LANES), jnp.uint32)
    dup_spec = pl.BlockSpec((tm // LANES, nrow, LANES), lambda i: (i, 0, 0))
    tab = jax.ShapeDtypeStruct((nrow, t), BF16)
    return pl.pallas_call(
        _route_kernel,
        grid=(t // tm,),
        in_specs=[pl.BlockSpec((tm, d), row), pl.BlockSpec((1, d), const),
                  pl.BlockSpec((None, 1, d), per_b), pl.BlockSpec((None, 1, d), per_b),
                  pl.BlockSpec(weff.shape, const)],
        out_specs=[pl.BlockSpec((d, tm), col), dup_spec, dup_spec] + [pl.BlockSpec((nrow, tm), col)] * 2,
        out_shape=[jax.ShapeDtypeStruct((d, t), BF16), dup, dup, tab, tab],
        scratch_shapes=[pltpu.VMEM((2 * nrow, tm), F32), pltpu.VMEM((nrow, LANES), F32),
                        pltpu.VMEM((nrow, LANES), F32),
                        pltpu.VMEM((2 * PEER_TOPK * PEER_HEADS, LANES), F32),
                        pltpu.VMEM((PEER_HEADS, LANES), F32)],
        compiler_params=_params("parallel"),
        name="peer_route",
    )(x2d, gain.reshape(1, d), scale, shift, weff)


def _experts_kernel(ht_ref, u_ref, v_ref, r2_ref, e2_ref, n_ref, e1_ref, x_ref, gate_ref, nf_ref,
                    o_ref, acc_ref, act_ref, *, final):
    j = pl.program_id(1)
    te = u_ref.shape[0]
    tm = ht_ref.shape[1]
    nk = PEER_NKEYS
    hp = PEER_HEADS
    chunk = 2 * nk
    wg = 2
    zero = jnp.zeros((), BF16)

    @pl.when(j == 0)
    def _():
        acc_ref[...] = jnp.zeros_like(acc_ref)

    def bcast_row(ref, r, lg):
        row = jnp.concatenate([ref[lg * wg + g, r:r + 1, :] for g in range(wg)], axis=1)
        return pltpu.bitcast(jnp.broadcast_to(row, (nk // 2, wg * LANES)), BF16)

    for c in range(te // chunk):
        hid = jnp.dot(u_ref[c * chunk:(c + 1) * chunk, :], ht_ref[...], preferred_element_type=F32)
        for lg in range(tm // (wg * LANES)):
            lanes = slice(lg * wg * LANES, (lg + 1) * wg * LANES)
            gacc = [None, None]
            for h in range(hp):
                r2 = r2_ref[h * nk:(h + 1) * nk, lanes]
                e2 = e2_ref[h * nk:(h + 1) * nk, lanes]
                for s2 in range(2):
                    r = (2 * c + s2) * hp + h
                    sel = jnp.where(r2 < bcast_row(n_ref, r, lg), e2, zero) * bcast_row(e1_ref, r, lg)
                    gacc[s2] = sel if gacc[s2] is None else gacc[s2] + sel
            for s2 in range(2):
                act_ref[c * chunk + s2 * nk: c * chunk + (s2 + 1) * nk, lanes] = (
                    _gelu(hid[s2 * nk:(s2 + 1) * nk, lanes].astype(BF16)) * gacc[s2])
    acc_ref[...] += lax.dot_general(v_ref[...], act_ref[...], (((0,), (0,)), ((), ())),
                                    preferred_element_type=F32)

    @pl.when(j == pl.num_programs(1) - 1)
    def _():
        y = x_ref[...] + gate_ref[...] * acc_ref[...].T
        if final:
            y = _rms(y, nf_ref[...])
        o_ref[...] = y


def _peer_experts(x2d, seq, gate, ht, n_tab, e1_tab, r2_tab, e2_tab, u, vt, norm_final, final):
    t, d = x2d.shape
    ne = u.shape[0]
    tm = min(1024, seq)
    te = 1024
    tpb = seq // tm
    nrow = PEER_NKEYS * PEER_HEADS
    srow = (te // PEER_NKEYS) * PEER_HEADS
    return pl.pallas_call(
        functools.partial(_experts_kernel, final=final),
        grid=(t // tm, ne // te),
        in_specs=[pl.BlockSpec((d, tm), lambda i, j: (0, i)),
                  pl.BlockSpec((te, d), lambda i, j: (j, 0)),
                  pl.BlockSpec((te, d), lambda i, j: (j, 0)),
                  pl.BlockSpec((nrow, tm), lambda i, j: (0, i)),
                  pl.BlockSpec((nrow, tm), lambda i, j: (0, i)),
                  pl.BlockSpec((tm // LANES, srow, LANES), lambda i, j: (i, j, 0)),
                  pl.BlockSpec((tm // LANES, srow, LANES), lambda i, j: (i, j, 0)),
                  pl.BlockSpec((tm, d), lambda i, j: (i, 0)),
                  pl.BlockSpec((None, 1, d), lambda i, j: (i // tpb, 0, 0)),
                  pl.BlockSpec((1, d), lambda i, j: (0, 0))],
        out_specs=pl.BlockSpec((tm, d), lambda i, j: (i, 0)),
        out_shape=jax.ShapeDtypeStruct((t, d), F32),
        scratch_shapes=[pltpu.VMEM((d, tm), F32), pltpu.VMEM((te, tm), BF16)],
        compiler_params=_params("parallel", "arbitrary"),
        name="peer_experts",
    )(ht, u, vt, r2_tab, e2_tab, n_tab, e1_tab, x2d, gate, norm_final.reshape(1, d))


def _peer_layer(x2d, seq, gain, shift, scale, gate, w_query, sub_keys, u, v, norm_final, final):
    weff = _peer_score_weights(w_query, sub_keys)
    ht, n_tab, e1_tab, r2_tab, e2_tab = _peer_route(x2d, seq, gain, shift, scale, weff)
    return _peer_experts(x2d, seq, gate, ht, n_tab, e1_tab, r2_tab, e2_tab,
                         u.astype(BF16), v.astype(BF16), norm_final, final)


def kernel(x, c, positions, norm_mix, norm_ffn, ada_w, ada_b, ret_w_in, ret_w_out, sgu_w_in, sgu_b_in,
           sgu_ln_g, sgu_ln_b, sgu_w_s, sgu_b_s, sgu_w_out, peer_w_query, peer_sub_keys, peer_u, peer_v,
           norm_final):
    bsz, seq, d = x.shape
    depth = ada_w.shape[0]
    mods = _mods(c, ada_w, ada_b)
    xt = x.reshape(bsz * seq, d)
    for layer in range(depth):
        sh1, sc1, g1, sh2, sc2, g2 = (mods[layer, :, i] for i in range(6))
        j = layer // 2
        if layer % 2 == 0:
            xt = _retention_layer(xt, positions, bsz, seq, norm_mix[layer], sh1, sc1, g1,
                                  ret_w_in[j], ret_w_out[j])
        else:
            xt = _sgu_layer(xt, seq, norm_mix[layer], sh1, sc1, g1, sgu_w_in[j], sgu_b_in[j],
                            sgu_ln_g[j], sgu_ln_b[j], sgu_w_s[j], sgu_b_s[j], sgu_w_out[j])
        xt = _peer_layer(xt, seq, norm_ffn[layer], sh2, sc2, g2, peer_w_query[layer], peer_sub_keys[layer],
                         peer_u[layer], peer_v[layer], norm_final, layer == depth - 1)
    return xt.reshape(bsz, seq, d)
```

```python
import functools
import math

import jax
import jax.numpy as jnp
from jax import lax
from jax.experimental import pallas as pl
from jax.experimental.pallas import tpu as pltpu

F32 = jnp.float32
BF16 = jnp.bfloat16

EPS = 1e-6
CHUNK = 64
RET_HEADS = 4
ROPE_BASE = 10000.0
SGU_BLOCK = 128
SGU_GROUPS = 8
PEER_HEADS = 8
PEER_NKEYS = 128
PEER_HALF = 128
PEER_TOPK = 16

LANES = 128
SUBLANES = 8
VMEM_LIMIT_BYTES = 56 * 1024 * 1024

RET_BLOCK = 256
NEG_INF = float("-inf")


def _params(*sem):
    return pltpu.CompilerParams(dimension_semantics=sem, vmem_limit_bytes=VMEM_LIMIT_BYTES)


_GELU_A1 = 2.0 * math.sqrt(2.0 / math.pi) * math.log2(math.e)


def _gelu(x):
    z = (x * x * (-_GELU_A1 * 0.044715) + (-_GELU_A1)) * x
    return x * (1.0 / (1.0 + jnp.exp2(z)))


def _silu(x):
    return x * (1.0 / (1.0 + jnp.exp(-x)))


def _rms(x, gain):
    return x * lax.rsqrt(jnp.mean(x * x, axis=-1, keepdims=True) + EPS) * gain


def _norm_mod(x, gain, scale, shift):
    return _rms(x, gain) * (1.0 + scale) + shift


def _mods_kernel(c_ref, w_ref, b_ref, o_ref):
    o_ref[...] = jnp.dot(_silu(c_ref[...]), w_ref[...], preferred_element_type=F32,
                         precision=lax.Precision.HIGHEST) + b_ref[...]


def _mods(c, ada_w, ada_b):
    depth, d, n = ada_w.shape
    bsz = c.shape[0]
    cp = jnp.zeros((SUBLANES, d), F32).at[:bsz].set(c)
    tn = 1536
    out = pl.pallas_call(
        _mods_kernel,
        grid=(depth, n // tn),
        in_specs=[pl.BlockSpec((SUBLANES, d), lambda l, j: (0, 0)),
                  pl.BlockSpec((None, d, tn), lambda l, j: (l, 0, j)),
                  pl.BlockSpec((None, 1, tn), lambda l, j: (l, 0, j))],
        out_specs=pl.BlockSpec((None, SUBLANES, tn), lambda l, j: (l, 0, j)),
        out_shape=jax.ShapeDtypeStruct((depth, SUBLANES, n), F32),
        compiler_params=_params("parallel", "parallel"),
        name="adaln_mods",
    )(cp, ada_w, ada_b.reshape(depth, 1, n))
    return out[:, :bsz].reshape(depth, bsz, 6, 1, d)


def _inproj_kernel(x_ref, pos_ref, invf_ref, gain_ref, sc_ref, sh_ref, w_ref,
                   q_ref, k_ref, v_ref, g_ref, *, d, dk):
    h = _norm_mod(x_ref[...], gain_ref[...], sc_ref[...], sh_ref[...]).astype(BF16)
    ang = pos_ref[...].astype(F32) * invf_ref[...]
    cos = jnp.cos(ang)
    sin = jnp.sin(ang)
    half = dk // 2
    for base, o_ref, scale in ((0, q_ref, None), (d, k_ref, dk ** -0.5)):
        for hd in range(d // dk):
            y = jnp.dot(h, w_ref[:, base + hd * dk: base + (hd + 1) * dk], preferred_element_type=F32)
            x1 = y[:, :half]
            x2 = y[:, half:]
            o1 = x1 * cos - x2 * sin
            o2 = x1 * sin + x2 * cos
            if scale is not None:
                o1 = o1 * scale
                o2 = o2 * scale
            o_ref[:, hd * dk: hd * dk + half] = o1.astype(BF16)
            o_ref[:, hd * dk + half: (hd + 1) * dk] = o2.astype(BF16)
    vw = v_ref.shape[1]
    cw = 512
    for base, o_ref in ((2 * d, v_ref), (2 * d + vw, g_ref)):
        for j in range(vw // cw):
            o_ref[:, j * cw:(j + 1) * cw] = jnp.dot(
                h, w_ref[:, base + j * cw: base + (j + 1) * cw], preferred_element_type=F32).astype(BF16)


def _ret_kernel(q_ref, k_ref, v_ref, g_ref, dm_ref, xi_ref, zt_ref, wout_ref, x_ref, gate_ref,
                o_ref, s_ref, *, dk, dv, gl):
    @pl.when(pl.program_id(1) == 0)
    def _():
        s_ref[...] = jnp.zeros_like(s_ref)

    proj = None
    for h in range(len(gl)):
        q = q_ref[:, h * dk:(h + 1) * dk]
        k = k_ref[:, h * dk:(h + 1) * dk]
        v = v_ref[:, h * dv:(h + 1) * dv]
        sc = lax.dot_general(q, k, (((1,), (1,)), ((), ())), preferred_element_type=F32) * dm_ref[h]
        intra = jnp.dot(sc.astype(BF16), v, preferred_element_type=F32)
        st = s_ref[h]
        qx = (q.astype(F32) * xi_ref[h]).astype(BF16)
        cross = jnp.dot(qx, st.astype(BF16), preferred_element_type=F32)
        kz = (k.astype(F32) * zt_ref[h]).astype(BF16)
        s_ref[h] = st * gl[h] + lax.dot_general(kz, v, (((0,), (0,)), ((), ())), preferred_element_type=F32)
        y = intra + cross
        yn = y * lax.rsqrt(jnp.mean(y * y, axis=-1, keepdims=True) + EPS)
        yg = (_silu(g_ref[:, h * dv:(h + 1) * dv].astype(F32)) * yn).astype(BF16)
        part = jnp.dot(yg, wout_ref[h * dv:(h + 1) * dv, :], preferred_element_type=F32)
        proj = part if proj is None else proj + part
    o_ref[...] = x_ref[...] + gate_ref[...] * proj


def _retention_tables(dk):
    lb = RET_BLOCK
    log_gamma = jnp.log(1.0 - 2.0 ** (-5.0 - jnp.arange(RET_HEADS, dtype=F32)))[:, None, None]
    idx = jnp.arange(lb, dtype=F32)
    diff = idx[:, None] - idx[None, :]
    ci = jnp.arange(lb)[:, None] // CHUNK
    cj = jnp.arange(lb)[None, :] // CHUNK
    dist = jnp.where(ci == cj, jnp.abs(diff), diff)
    dm = jnp.where((cj <= ci)[None], jnp.exp(log_gamma * dist[None]), 0.0)
    xi = jnp.broadcast_to(jnp.exp(log_gamma * (idx[None, :, None] + 1.0)), (RET_HEADS, lb, dk))
    zt = jnp.broadcast_to(jnp.exp(log_gamma * (lb - 1.0 - idx[None, :, None])), (RET_HEADS, lb, dk))
    gl = tuple(float((1.0 - 2.0 ** (-5.0 - h)) ** lb) for h in range(RET_HEADS))
    return dm, xi, zt, gl


def _retention_layer(x2d, positions, bsz, seq, gain, shift, scale, gate, w_in, w_out):
    t, d = x2d.shape
    dk = d // RET_HEADS
    dv = 2 * dk
    vw = RET_HEADS * dv
    tm = min(512, seq)
    tpb = seq // tm
    half = dk // 2
    inv_freq = (1.0 / (ROPE_BASE ** (jnp.arange(half, dtype=F32) / half))).reshape(1, half)
    row = lambda i: (i, 0)
    per_b = lambda i: (i // tpb, 0, 0)
    const = lambda i: (0, 0)
    q, k, v, g = pl.pallas_call(
        functools.partial(_inproj_kernel, d=d, dk=dk),
        grid=(t // tm,),
        in_specs=[pl.BlockSpec((tm, d), row),
                  pl.BlockSpec((tm, 1), row),
                  pl.BlockSpec((1, half), const),
                  pl.BlockSpec((1, d), const),
                  pl.BlockSpec((None, 1, d), per_b),
                  pl.BlockSpec((None, 1, d), per_b),
                  pl.BlockSpec(w_in.shape, const)],
        out_specs=[pl.BlockSpec((tm, d), row), pl.BlockSpec((tm, d), row),
                   pl.BlockSpec((tm, vw), row), pl.BlockSpec((tm, vw), row)],
        out_shape=[jax.ShapeDtypeStruct((t, d), BF16), jax.ShapeDtypeStruct((t, d), BF16),
                   jax.ShapeDtypeStruct((t, vw), BF16), jax.ShapeDtypeStruct((t, vw), BF16)],
        compiler_params=_params("parallel"),
        name="ret_inproj_rope",
    )(x2d, positions.reshape(t, 1), inv_freq, gain.reshape(1, d), scale, shift, w_in.astype(BF16))

    dm, xi, zt, gl = _retention_tables(dk)
    lb = RET_BLOCK
    nblk = seq // lb
    blk = lambda b, i: (b * nblk + i, 0)
    full3 = lambda b, i: (0, 0, 0)
    return pl.pallas_call(
        functools.partial(_ret_kernel, dk=dk, dv=dv, gl=gl),
        grid=(bsz, nblk),
        in_specs=[pl.BlockSpec((lb, d), blk), pl.BlockSpec((lb, d), blk),
                  pl.BlockSpec((lb, vw), blk), pl.BlockSpec((lb, vw), blk),
                  pl.BlockSpec(dm.shape, full3), pl.BlockSpec(xi.shape, full3), pl.BlockSpec(zt.shape, full3),
                  pl.BlockSpec((vw, d), lambda b, i: (0, 0)),
                  pl.BlockSpec((lb, d), blk),
                  pl.BlockSpec((None, 1, d), lambda b, i: (b, 0, 0))],
        out_specs=pl.BlockSpec((lb, d), blk),
        out_shape=jax.ShapeDtypeStruct((t, d), F32),
        scratch_shapes=[pltpu.VMEM((RET_HEADS, dk, dv), F32)],
        compiler_params=_params("parallel", "arbitrary"),
        name="ret_scan_outproj",
    )(q, k, v, g, dm, xi, zt, w_out.astype(BF16), x2d, gate)


def _sgu_kernel(x_ref, gain_ref, sc_ref, sh_ref, gate_ref, win_ref, bin_ref, lng_ref, lnb_ref,
                ws_ref, bst_ref, wout_ref, o_ref, *, half, gdim, sub):
    pi = lax.broadcasted_iota(jnp.int32, (SGU_BLOCK, SGU_BLOCK), 0) // CHUNK
    pj = lax.broadcasted_iota(jnp.int32, (SGU_BLOCK, SGU_BLOCK), 1) // CHUNK
    keep = pj <= pi
    wm = [jnp.where(keep, ws_ref[g], 0.0).astype(BF16) for g in range(SGU_GROUPS)]
    for r in range(x_ref.shape[0] // sub):
        rs = slice(r * sub, (r + 1) * sub)
        x = x_ref[rs, :]
        h = _norm_mod(x, gain_ref[...], sc_ref[...], sh_ref[...]).astype(BF16)
        u = _gelu((jnp.dot(h, win_ref[:, :half], preferred_element_type=F32) + bin_ref[:, :half]).astype(BF16))
        v = _gelu(jnp.dot(h, win_ref[:, half:], preferred_element_type=F32) + bin_ref[:, half:])
        mu = jnp.mean(v, axis=-1, keepdims=True)
        vc = v - mu
        v = (vc * lax.rsqrt(jnp.mean(vc * vc, axis=-1, keepdims=True) + EPS) * lng_ref[...]
             + lnb_ref[...]).astype(BF16)
        blocks = []
        for n in range(sub // SGU_BLOCK):
            cols = []
            for g in range(SGU_GROUPS):
                vb = v[n * SGU_BLOCK:(n + 1) * SGU_BLOCK, g * gdim:(g + 1) * gdim]
                cols.append(jnp.dot(wm[g], vb, preferred_element_type=F32) + bst_ref[:, g:g + 1])
            blocks.append(jnp.concatenate(cols, axis=1))
        s = blocks[0] if len(blocks) == 1 else jnp.concatenate(blocks, axis=0)
        y = jnp.dot(u * s.astype(BF16), wout_ref[...], preferred_element_type=F32)
        o_ref[rs, :] = x + gate_ref[...] * y


def _sgu_layer(x2d, seq, gain, shift, scale, gate, w_in, b_in, ln_g, ln_b, w_s, b_s, w_out):
    t, d = x2d.shape
    ffn = w_in.shape[1]
    half = ffn // 2
    tm = min(512, seq)
    sub = min(2 * SGU_BLOCK, tm)
    tpb = seq // tm
    row = lambda i: (i, 0)
    per_b = lambda i: (i // tpb, 0, 0)
    const = lambda i: (0, 0)
    once = pl.Buffered(1)
    return pl.pallas_call(
        functools.partial(_sgu_kernel, half=half, gdim=half // SGU_GROUPS, sub=sub),
        grid=(t // tm,),
        in_specs=[pl.BlockSpec((tm, d), row),
                  pl.BlockSpec((1, d), const),
                  pl.BlockSpec((None, 1, d), per_b), pl.BlockSpec((None, 1, d), per_b),
                  pl.BlockSpec((None, 1, d), per_b),
                  pl.BlockSpec((d, ffn), const, pipeline_mode=once), pl.BlockSpec((1, ffn), const),
                  pl.BlockSpec((1, half), const), pl.BlockSpec((1, half), const),
                  pl.BlockSpec(w_s.shape, lambda i: (0, 0, 0)),
                  pl.BlockSpec((SGU_BLOCK, SGU_GROUPS), const),
                  pl.BlockSpec((half, d), const, pipeline_mode=once)],
        out_specs=pl.BlockSpec((tm, d), row),
        out_shape=jax.ShapeDtypeStruct((t, d), F32),
        compiler_params=_params("parallel"),
        name="sgu_mixer",
    )(x2d, gain.reshape(1, d), scale, shift, gate, w_in.astype(BF16), b_in.reshape(1, ffn),
      ln_g.reshape(1, half), ln_b.reshape(1, half), w_s, b_s.T, w_out.astype(BF16))


def _weff_kernel(keys_ref, wq_ref, o_ref):
    o_ref[0] = lax.dot_general(keys_ref[0, 0], wq_ref[...], (((1,), (1,)), ((), ())),
                               preferred_element_type=F32, precision=lax.Precision.HIGHEST)


def _peer_score_weights(w_query, sub_keys):
    d = w_query.shape[0]
    nhp = PEER_HEADS * 2
    out = pl.pallas_call(
        _weff_kernel,
        grid=(nhp,),
        in_specs=[pl.BlockSpec((1, 1, PEER_NKEYS, PEER_HALF), lambda i: (i // 2, i % 2, 0, 0)),
                  pl.BlockSpec((d, PEER_HALF), lambda i: (0, i))],
        out_specs=pl.BlockSpec((1, PEER_NKEYS, d), lambda i: (i, 0, 0)),
        out_shape=jax.ShapeDtypeStruct((nhp, PEER_NKEYS, d), F32),
        compiler_params=_params("parallel"),
        name="peer_score_weights",
    )(sub_keys, w_query)
    out = out.reshape(PEER_HEADS, 2, PEER_NKEYS, d).transpose(1, 2, 0, 3)
    return out.reshape(2 * PEER_NKEYS * PEER_HEADS, d).astype(BF16)


def _dup_bf16(x):
    b = pltpu.bitcast(x, jnp.uint32)
    hi = (b + jnp.uint32(0x7FFF) + ((b >> 16) & jnp.uint32(1))) >> 16
    return hi | (hi << 16)


def _lex_first(vx, ix, vy, iy):
    return (vx > vy) | ((vx == vy) & (ix < iy))


def _ce(x, y):
    f = _lex_first(x[0], x[1], y[0], y[1])
    return ((jnp.where(f, x[0], y[0]), jnp.where(f, x[1], y[1])),
            (jnp.where(f, y[0], x[0]), jnp.where(f, y[1], x[1])))


def _bitonic_merge(z):
    n = len(z)
    j = n // 2
    while j >= 1:
        for i in range(n):
            l = i ^ j
            if l > i:
                z[i], z[l] = _ce(z[i], z[l])
        j //= 2
    return z


def _sort16(z):
    n = len(z)
    k = 2
    while k <= n:
        j = k // 2
        while j >= 1:
            for i in range(n):
                l = i ^ j
                if l > i:
                    if (i & k) == 0:
                        z[i], z[l] = _ce(z[i], z[l])
                    else:
                        z[l], z[i] = _ce(z[i], z[l])
            j //= 2
        k *= 2
    return z


def _top16_pairs(load, nkeys):
    k = PEER_TOPK
    top = None
    for g in range(nkeys // k):
        grp = _sort16([load(g * k + i) for i in range(k)])
        if top is None:
            top = grp
        else:
            z = []
            for i in range(k):
                a, b = top[i], grp[k - 1 - i]
                f = _lex_first(a[0], a[1], b[0], b[1])
                z.append((jnp.where(f, a[0], b[0]), jnp.where(f, a[1], b[1])))
            top = _bitonic_merge(z)
    return top


def _max_merge(z):
    n = len(z)
    j = n // 2
    while j >= 1:
        for i in range(n):
            l = i ^ j
            if l > i:
                z[i], z[l] = jnp.maximum(z[i], z[l]), jnp.minimum(z[i], z[l])
        j //= 2
    return z


def _sort16_vals(z):
    n = len(z)
    k = 2
    while k <= n:
        j = k // 2
        while j >= 1:
            for i in range(n):
                l = i ^ j
                if l > i:
                    hi, lo = jnp.maximum(z[i], z[l]), jnp.minimum(z[i], z[l])
                    z[i], z[l] = (hi, lo) if (i & k) == 0 else (lo, hi)
            j //= 2
        k *= 2
    return z


def _top16_vals(load, nkeys):
    k = PEER_TOPK
    top = None
    for g in range(nkeys // k):
        grp = _sort16_vals([load(g * k + i) for i in range(k)])
        top = grp if top is None else _max_merge([jnp.maximum(top[i], grp[k - 1 - i]) for i in range(k)])
    return top


def _route_kernel(x_ref, gain_ref, sc_ref, sh_ref, weff_ref,
                  ht_ref, n_ref, e1_ref, r2_ref, e2_ref, s_ref, r2s_ref, e2s_ref, m_ref, mode_ref):
    tm = x_ref.shape[0]
    k = PEER_TOPK
    nk = PEER_NKEYS
    hp = PEER_HEADS
    h2 = _norm_mod(x_ref[...], gain_ref[...], sc_ref[...], sh_ref[...])
    h2t = h2.T.astype(BF16)
    ht_ref[...] = h2t
    s_ref[...] = jnp.dot(weff_ref[...], h2t, preferred_element_type=F32)

    def group(gi, carry):
        lane = pl.ds(pl.multiple_of(gi * LANES, LANES), LANES)

        def score(p, key):
            r0 = (p * nk + key) * hp
            return s_ref[r0:r0 + hp, lane]

        v = [_top16_vals(functools.partial(score, p), nk) for p in range(2)]
        ties = jnp.zeros((hp, LANES), F32)
        for p in range(2):
            for b in range(k - 1):
                ties = ties + jnp.where(v[p][b] == v[p][b + 1], 1.0, 0.0)
            ge = jnp.zeros((hp, LANES), F32)
            for key in range(nk):
                ge = ge + jnp.where(score(p, key) >= v[p][k - 1], 1.0, 0.0)
            ties = ties + jnp.where(ge > float(k), 1.0, 0.0)
        for p in range(2):
            for a in range(k):
                m_ref[(p * k + a) * hp:(p * k + a + 1) * hp, :] = v[p][a]
        mode_ref[...] = jnp.zeros((hp, LANES), F32)

        @pl.when(jnp.max(ties) > 0.0)
        def _():
            for p in range(2):
                top = _top16_pairs(
                    lambda key: (score(p, key), jnp.full((hp, LANES), float(key), F32)), nk)
                for a in range(k):
                    m_ref[(p * k + a) * hp:(p * k + a + 1) * hp, :] = top[a][1]
            mode_ref[...] = jnp.ones((hp, LANES), F32)

        v1, v2 = v

        ln = [k // (a + 1) for a in range(k)]
        c = [[v1[a] + v2[b] for b in range(ln[a])] for a in range(k)]
        top = list(c[0])
        for a in range(1, k):
            for i in range(k - ln[a], k):
                top[i] = jnp.maximum(top[i], c[a][k - 1 - i])
            top = _max_merge(top)
        tau = top[k - 1]
        cnt_gt = []
        cnt_eq = []
        for a in range(k):
            gt = sum(jnp.where(c[a][b] > tau, 1.0, 0.0) for b in range(ln[a]))
            ge = sum(jnp.where(c[a][b] >= tau, 1.0, 0.0) for b in range(ln[a]))
            cnt_gt.append(gt)
            cnt_eq.append(ge - gt)
        rem = float(k) - sum(cnt_gt)
        n_sel = []
        for a in range(k):
            n_sel.append(cnt_gt[a] + jnp.clip(rem, 0.0, cnt_eq[a]))
            rem = rem - cnt_eq[a]
        m = c[0][0]
        z = 0.0
        for a in range(k):
            for b in range(ln[a]):
                z = z + jnp.where(n_sel[a] > float(b), jnp.exp(c[a][b] - m), 0.0)
        inv_z = 1.0 / z

        by_index = mode_ref[...] > 0.5

        def first_key_body(key, carry2):
            rows = pl.ds(pl.multiple_of(key * hp, hp), hp)
            s1 = s_ref[rows, lane]
            probe = jnp.where(by_index, jnp.asarray(key).astype(F32), s1)
            nacc = jnp.zeros((hp, LANES), F32)
            for a in range(k):
                nacc = jnp.where(m_ref[a * hp:(a + 1) * hp, :] == probe, n_sel[a], nacc)
            n_ref[gi, rows, :] = _dup_bf16(nacc)
            e1_ref[gi, rows, :] = _dup_bf16(jnp.exp(s1 - v1[0]) * inv_z)
            return carry2

        def second_key_body(key, carry2):
            rows = pl.ds(pl.multiple_of(key * hp, hp), hp)
            s2 = s_ref[pl.ds(pl.multiple_of(nk * hp + key * hp, hp), hp), lane]
            probe = jnp.where(by_index, jnp.asarray(key).astype(F32), s2)
            racc = jnp.full((hp, LANES), float(k), F32)
            for b in range(k):
                racc = jnp.where(m_ref[(k + b) * hp:(k + b + 1) * hp, :] == probe, float(b), racc)
            r2s_ref[rows, :] = racc
            e2s_ref[rows, :] = jnp.exp(s2 - v2[0])
            return carry2

        lax.fori_loop(0, nk, first_key_body, 0, unroll=8)
        lax.fori_loop(0, nk, second_key_body, 0, unroll=8)
        for h in range(hp):
            r2_ref[h * nk:(h + 1) * nk, lane] = r2s_ref[pl.ds(h, nk, stride=hp), :].astype(BF16)
            e2_ref[h * nk:(h + 1) * nk, lane] = e2s_ref[pl.ds(h, nk, stride=hp), :].astype(BF16)
        return carry

    lax.fori_loop(0, tm // LANES, group, 0)


def _peer_route(x2d, seq, gain, shift, scale, weff):
    t, d = x2d.shape
    tm = min(512, seq)
    tpb = seq // tm
    nrow = PEER_NKEYS * PEER_HEADS
    row = lambda i: (i, 0)
    col = lambda i: (0, i)
    per_b = lambda i: (i // tpb, 0, 0)
    const = lambda i: (0, 0)
    dup = jax.ShapeDtypeStruct((t // LANES, nrow, LANES), jnp.uint32)
    dup_spec = pl.BlockSpec((tm // LANES, nrow, LANES), lambda i: (i, 0, 0))
    tab = jax.ShapeDtypeStruct((nrow, t), BF16)
    return pl.pallas_call(
        _route_kernel,
        grid=(t // tm,),
        in_specs=[pl.BlockSpec((tm, d), row), pl.BlockSpec((1, d), const),
                  pl.BlockSpec((None, 1, d), per_b), pl.BlockSpec((None, 1, d), per_b),
                  pl.BlockSpec(weff.shape, const)],
        out_specs=[pl.BlockSpec((d, tm), col), dup_spec, dup_spec] + [pl.BlockSpec((nrow, tm), col)] * 2,
        out_shape=[jax.ShapeDtypeStruct((d, t), BF16), dup, dup, tab, tab],
        scratch_shapes=[pltpu.VMEM((2 * nrow, tm), F32), pltpu.VMEM((nrow, LANES), F32),
                        pltpu.VMEM((nrow, LANES), F32),
                        pltpu.VMEM((2 * PEER_TOPK * PEER_HEADS, LANES), F32),
                        pltpu.VMEM((PEER_HEADS, LANES), F32)],
        compiler_params=_params("parallel"),
        name="peer_route",
    )(x2d, gain.reshape(1, d), scale, shift, weff)


def _experts_kernel(ht_ref, u_ref, v_ref, r2_ref, e2_ref, n_ref, e1_ref, x_ref, gate_ref, nf_ref,
                    o_ref, acc_ref, act_ref, *, final):
    j = pl.program_id(1)
    te = u_ref.shape[0]
    tm = ht_ref.shape[1]
    nk = PEER_NKEYS
    hp = PEER_HEADS
    chunk = 2 * nk
    wg = 2
    zero = jnp.zeros((), BF16)

    @pl.when(j == 0)
    def _():
        acc_ref[...] = jnp.zeros_like(acc_ref)

    def bcast_row(ref, r, lg):
        row = jnp.concatenate([ref[lg * wg + g, r:r + 1, :] for g in range(wg)], axis=1)
        return pltpu.bitcast(jnp.broadcast_to(row, (nk // 2, wg * LANES)), BF16)

    for c in range(te // chunk):
        hid = jnp.dot(u_ref[c * chunk:(c + 1) * chunk, :], ht_ref[...], preferred_element_type=F32)
        for lg in range(tm // (wg * LANES)):
            lanes = slice(lg * wg * LANES, (lg + 1) * wg * LANES)
            gacc = [None, None]
            for h in range(hp):
                r2 = r2_ref[h * nk:(h + 1) * nk, lanes]
                e2 = e2_ref[h * nk:(h + 1) * nk, lanes]
                for s2 in range(2):
                    r = (2 * c + s2) * hp + h
                    sel = jnp.where(r2 < bcast_row(n_ref, r, lg), e2, zero) * bcast_row(e1_ref, r, lg)
                    gacc[s2] = sel if gacc[s2] is None else gacc[s2] + sel
            for s2 in range(2):
                act_ref[c * chunk + s2 * nk: c * chunk + (s2 + 1) * nk, lanes] = (
                    _gelu(hid[s2 * nk:(s2 + 1) * nk, lanes].astype(BF16)) * gacc[s2])
    acc_ref[...] += lax.dot_general(v_ref[...], act_ref[...], (((0,), (0,)), ((), ())),
                                    preferred_element_type=F32)

    @pl.when(j == pl.num_programs(1) - 1)
    def _():
        y = x_ref[...] + gate_ref[...] * acc_ref[...].T
        if final:
            y = _rms(y, nf_ref[...])
        o_ref[...] = y


def _peer_experts(x2d, seq, gate, ht, n_tab, e1_tab, r2_tab, e2_tab, u_all, v_all, layer, norm_final, final):
    t, d = x2d.shape
    ne = u_all.shape[1]
    tm = min(1024, seq)
    te = 1024
    tpb = seq // tm
    nrow = PEER_NKEYS * PEER_HEADS
    srow = (te // PEER_NKEYS) * PEER_HEADS
    return pl.pallas_call(
        functools.partial(_experts_kernel, final=final),
        grid=(t // tm, ne // te),
        in_specs=[pl.BlockSpec((d, tm), lambda i, j: (0, i)),
                  pl.BlockSpec((None, te, d), lambda i, j: (layer, j, 0)),
                  pl.BlockSpec((None, te, d), lambda i, j: (layer, j, 0)),
                  pl.BlockSpec((nrow, tm), lambda i, j: (0, i)),
                  pl.BlockSpec((nrow, tm), lambda i, j: (0, i)),
                  pl.BlockSpec((tm // LANES, srow, LANES), lambda i, j: (i, j, 0)),
                  pl.BlockSpec((tm // LANES, srow, LANES), lambda i, j: (i, j, 0)),
                  pl.BlockSpec((tm, d), lambda i, j: (i, 0)),
                  pl.BlockSpec((None, 1, d), lambda i, j: (i // tpb, 0, 0)),
                  pl.BlockSpec((1, d), lambda i, j: (0, 0))],
        out_specs=pl.BlockSpec((tm, d), lambda i, j: (i, 0)),
        out_shape=jax.ShapeDtypeStruct((t, d), F32),
        scratch_shapes=[pltpu.VMEM((d, tm), F32), pltpu.VMEM((te, tm), BF16)],
        compiler_params=_params("parallel", "arbitrary"),
        name="peer_experts",
    )(ht, u_all, v_all, r2_tab, e2_tab, n_tab, e1_tab, x2d, gate, norm_final.reshape(1, d))


def _peer_layer(x2d, seq, gain, shift, scale, gate, w_query, sub_keys, u_all, v_all, layer, norm_final, final):
    weff = _peer_score_weights(w_query, sub_keys)
    ht, n_tab, e1_tab, r2_tab, e2_tab = _peer_route(x2d, seq, gain, shift, scale, weff)
    return _peer_experts(x2d, seq, gate, ht, n_tab, e1_tab, r2_tab, e2_tab,
                         u_all, v_all, layer, norm_final, final)


def kernel(x, c, positions, norm_mix, norm_ffn, ada_w, ada_b, ret_w_in, ret_w_out, sgu_w_in, sgu_b_in,
           sgu_ln_g, sgu_ln_b, sgu_w_s, sgu_b_s, sgu_w_out, peer_w_query, peer_sub_keys, peer_u, peer_v,
           norm_final):
    bsz, seq, d = x.shape
    depth = ada_w.shape[0]
    mods = _mods(c, ada_w, ada_b)
    xt = x.reshape(bsz * seq, d)
    u_all = peer_u.astype(BF16)
    v_all = peer_v.astype(BF16)
    for layer in range(depth):
        sh1, sc1, g1, sh2, sc2, g2 = (mods[layer, :, i] for i in range(6))
        j = layer // 2
        if layer % 2 == 0:
            xt = _retention_layer(xt, positions, bsz, seq, norm_mix[layer], sh1, sc1, g1,
                                  ret_w_in[j], ret_w_out[j])
        else:
            xt = _sgu_layer(xt, seq, norm_mix[layer], sh1, sc1, g1, sgu_w_in[j], sgu_b_in[j],
                            sgu_ln_g[j], sgu_ln_b[j], sgu_w_s[j], sgu_b_s[j], sgu_w_out[j])
        xt = _peer_layer(xt, seq, norm_ffn[layer], sh2, sc2, g2, peer_w_query[layer], peer_sub_keys[layer],
                         u_all, v_all, layer, norm_final, layer == depth - 1)
    return xt.reshape(bsz, seq, d)
```

```python
import functools
import math

import jax
import jax.numpy as jnp
from jax import lax
from jax.experimental import pallas as pl
from jax.experimental.pallas import tpu as pltpu

F32 = jnp.float32
BF16 = jnp.bfloat16

EPS = 1e-6
CHUNK = 64
RET_HEADS = 4
ROPE_BASE = 10000.0
SGU_BLOCK = 128
SGU_GROUPS = 8
PEER_HEADS = 8
PEER_NKEYS = 128
PEER_HALF = 128
PEER_TOPK = 16

LANES = 128
SUBLANES = 8
VMEM_LIMIT_BYTES = 56 * 1024 * 1024

RET_BLOCK = 256
INPROJ_TOKENS = 512
SGU_TOKENS = 512
ROUTE_TOKENS = 512
EXPERT_TOKENS = 1024
EXPERT_TILE = 1024
ADALN_COLS = 1536


def _params(*sem):
    return pltpu.CompilerParams(dimension_semantics=sem, vmem_limit_bytes=VMEM_LIMIT_BYTES)


_GELU_A1 = 2.0 * math.sqrt(2.0 / math.pi) * math.log2(math.e)


def _gelu(x):
    z = (x * x * (-_GELU_A1 * 0.044715) + (-_GELU_A1)) * x
    return x * (1.0 / (1.0 + jnp.exp2(z)))


def _silu(x):
    return x * (1.0 / (1.0 + jnp.exp(-x)))


def _rms(x, gain):
    return x * lax.rsqrt(jnp.mean(x * x, axis=-1, keepdims=True) + EPS) * gain


def _norm_mod(x, gain, scale, shift):
    return _rms(x, gain) * (1.0 + scale) + shift


def _mods_kernel(c_ref, w_ref, b_ref, o_ref):
    o_ref[...] = jnp.dot(_silu(c_ref[...]), w_ref[...], preferred_element_type=F32,
                         precision=lax.Precision.HIGHEST) + b_ref[...]


def _mods(c, ada_w, ada_b):
    depth, d, n = ada_w.shape
    bsz = c.shape[0]
    cp = jnp.zeros((SUBLANES, d), F32).at[:bsz].set(c)
    tn = ADALN_COLS
    out = pl.pallas_call(
        _mods_kernel,
        grid=(depth, n // tn),
        in_specs=[pl.BlockSpec((SUBLANES, d), lambda l, j: (0, 0)),
                  pl.BlockSpec((None, d, tn), lambda l, j: (l, 0, j)),
                  pl.BlockSpec((None, 1, tn), lambda l, j: (l, 0, j))],
        out_specs=pl.BlockSpec((None, SUBLANES, tn), lambda l, j: (l, 0, j)),
        out_shape=jax.ShapeDtypeStruct((depth, SUBLANES, n), F32),
        compiler_params=_params("parallel", "parallel"),
        name="adaln_mods",
    )(cp, ada_w, ada_b.reshape(depth, 1, n))
    return out[:, :bsz].reshape(depth, bsz, 6, 1, d)


def _inproj_kernel(x_ref, pos_ref, invf_ref, gain_ref, sc_ref, sh_ref, w_ref,
                   q_ref, k_ref, v_ref, g_ref, *, d, dk):
    h = _norm_mod(x_ref[...], gain_ref[...], sc_ref[...], sh_ref[...]).astype(BF16)
    ang = pos_ref[...].astype(F32) * invf_ref[...]
    cos = jnp.cos(ang)
    sin = jnp.sin(ang)
    half = dk // 2
    for base, o_ref, scale in ((0, q_ref, None), (d, k_ref, dk ** -0.5)):
        for hd in range(d // dk):
            y = jnp.dot(h, w_ref[:, base + hd * dk: base + (hd + 1) * dk], preferred_element_type=F32)
            x1 = y[:, :half]
            x2 = y[:, half:]
            o1 = x1 * cos - x2 * sin
            o2 = x1 * sin + x2 * cos
            if scale is not None:
                o1 = o1 * scale
                o2 = o2 * scale
            o_ref[:, hd * dk: hd * dk + half] = o1.astype(BF16)
            o_ref[:, hd * dk + half: (hd + 1) * dk] = o2.astype(BF16)
    vw = v_ref.shape[1]
    cw = 512
    for base, o_ref in ((2 * d, v_ref), (2 * d + vw, g_ref)):
        for j in range(vw // cw):
            o_ref[:, j * cw:(j + 1) * cw] = jnp.dot(
                h, w_ref[:, base + j * cw: base + (j + 1) * cw], preferred_element_type=F32).astype(BF16)


def _ret_kernel(q_ref, k_ref, v_ref, g_ref, dm_ref, xi_ref, zt_ref, wout_ref, x_ref, gate_ref,
                o_ref, s_ref, *, dk, dv, gl):
    @pl.when(pl.program_id(1) == 0)
    def _():
        s_ref[...] = jnp.zeros_like(s_ref)

    proj = None
    for h in range(len(gl)):
        q = q_ref[:, h * dk:(h + 1) * dk]
        k = k_ref[:, h * dk:(h + 1) * dk]
        v = v_ref[:, h * dv:(h + 1) * dv]
        sc = lax.dot_general(q, k, (((1,), (1,)), ((), ())), preferred_element_type=F32) * dm_ref[h]
        intra = jnp.dot(sc.astype(BF16), v, preferred_element_type=F32)
        st = s_ref[h]
        qx = (q.astype(F32) * xi_ref[h]).astype(BF16)
        cross = jnp.dot(qx, st.astype(BF16), preferred_element_type=F32)
        kz = (k.astype(F32) * zt_ref[h]).astype(BF16)
        s_ref[h] = st * gl[h] + lax.dot_general(kz, v, (((0,), (0,)), ((), ())), preferred_element_type=F32)
        y = intra + cross
        yn = y * lax.rsqrt(jnp.mean(y * y, axis=-1, keepdims=True) + EPS)
        yg = (_silu(g_ref[:, h * dv:(h + 1) * dv].astype(F32)) * yn).astype(BF16)
        part = jnp.dot(yg, wout_ref[h * dv:(h + 1) * dv, :], preferred_element_type=F32)
        proj = part if proj is None else proj + part
    o_ref[...] = x_ref[...] + gate_ref[...] * proj


def _retention_tables(dk):
    lb = RET_BLOCK
    log_gamma = jnp.log(1.0 - 2.0 ** (-5.0 - jnp.arange(RET_HEADS, dtype=F32)))[:, None, None]
    idx = jnp.arange(lb, dtype=F32)
    diff = idx[:, None] - idx[None, :]
    ci = jnp.arange(lb)[:, None] // CHUNK
    cj = jnp.arange(lb)[None, :] // CHUNK
    dist = jnp.where(ci == cj, jnp.abs(diff), diff)
    dm = jnp.where((cj <= ci)[None], jnp.exp(log_gamma * dist[None]), 0.0)
    xi = jnp.broadcast_to(jnp.exp(log_gamma * (idx[None, :, None] + 1.0)), (RET_HEADS, lb, dk))
    zt = jnp.broadcast_to(jnp.exp(log_gamma * (lb - 1.0 - idx[None, :, None])), (RET_HEADS, lb, dk))
    gl = tuple(float((1.0 - 2.0 ** (-5.0 - h)) ** lb) for h in range(RET_HEADS))
    return dm, xi, zt, gl


def _retention_layer(x2d, positions, bsz, seq, gain, shift, scale, gate, w_in, w_out):
    t, d = x2d.shape
    dk = d // RET_HEADS
    dv = 2 * dk
    vw = RET_HEADS * dv
    tm = min(INPROJ_TOKENS, seq)
    tpb = seq // tm
    half = dk // 2
    inv_freq = (1.0 / (ROPE_BASE ** (jnp.arange(half, dtype=F32) / half))).reshape(1, half)
    row = lambda i: (i, 0)
    per_b = lambda i: (i // tpb, 0, 0)
    const = lambda i: (0, 0)
    q, k, v, g = pl.pallas_call(
        functools.partial(_inproj_kernel, d=d, dk=dk),
        grid=(t // tm,),
        in_specs=[pl.BlockSpec((tm, d), row),
                  pl.BlockSpec((tm, 1), row),
                  pl.BlockSpec((1, half), const),
                  pl.BlockSpec((1, d), const),
                  pl.BlockSpec((None, 1, d), per_b),
                  pl.BlockSpec((None, 1, d), per_b),
                  pl.BlockSpec(w_in.shape, const)],
        out_specs=[pl.BlockSpec((tm, d), row), pl.BlockSpec((tm, d), row),
                   pl.BlockSpec((tm, vw), row), pl.BlockSpec((tm, vw), row)],
        out_shape=[jax.ShapeDtypeStruct((t, d), BF16), jax.ShapeDtypeStruct((t, d), BF16),
                   jax.ShapeDtypeStruct((t, vw), BF16), jax.ShapeDtypeStruct((t, vw), BF16)],
        compiler_params=_params("parallel"),
        name="ret_inproj_rope",
    )(x2d, positions.reshape(t, 1), inv_freq, gain.reshape(1, d), scale, shift, w_in.astype(BF16))

    dm, xi, zt, gl = _retention_tables(dk)
    lb = RET_BLOCK
    nblk = seq // lb
    blk = lambda b, i: (b * nblk + i, 0)
    full3 = lambda b, i: (0, 0, 0)
    return pl.pallas_call(
        functools.partial(_ret_kernel, dk=dk, dv=dv, gl=gl),
        grid=(bsz, nblk),
        in_specs=[pl.BlockSpec((lb, d), blk), pl.BlockSpec((lb, d), blk),
                  pl.BlockSpec((lb, vw), blk), pl.BlockSpec((lb, vw), blk),
                  pl.BlockSpec(dm.shape, full3), pl.BlockSpec(xi.shape, full3), pl.BlockSpec(zt.shape, full3),
                  pl.BlockSpec((vw, d), lambda b, i: (0, 0)),
                  pl.BlockSpec((lb, d), blk),
                  pl.BlockSpec((None, 1, d), lambda b, i: (b, 0, 0))],
        out_specs=pl.BlockSpec((lb, d), blk),
        out_shape=jax.ShapeDtypeStruct((t, d), F32),
        scratch_shapes=[pltpu.VMEM((RET_HEADS, dk, dv), F32)],
        compiler_params=_params("parallel", "arbitrary"),
        name="ret_scan_outproj",
    )(q, k, v, g, dm, xi, zt, w_out.astype(BF16), x2d, gate)


def _sgu_kernel(x_ref, gain_ref, sc_ref, sh_ref, gate_ref, win_ref, bin_ref, lng_ref, lnb_ref,
                ws_ref, bst_ref, wout_ref, o_ref, *, half, gdim, sub):
    pi = lax.broadcasted_iota(jnp.int32, (SGU_BLOCK, SGU_BLOCK), 0) // CHUNK
    pj = lax.broadcasted_iota(jnp.int32, (SGU_BLOCK, SGU_BLOCK), 1) // CHUNK
    keep = pj <= pi
    wm = [jnp.where(keep, ws_ref[g], 0.0).astype(BF16) for g in range(SGU_GROUPS)]
    for r in range(x_ref.shape[0] // sub):
        rs = slice(r * sub, (r + 1) * sub)
        x = x_ref[rs, :]
        h = _norm_mod(x, gain_ref[...], sc_ref[...], sh_ref[...]).astype(BF16)
        u = _gelu((jnp.dot(h, win_ref[:, :half], preferred_element_type=F32) + bin_ref[:, :half]).astype(BF16))
        v = _gelu(jnp.dot(h, win_ref[:, half:], preferred_element_type=F32) + bin_ref[:, half:])
        mu = jnp.mean(v, axis=-1, keepdims=True)
        vc = v - mu
        v = (vc * lax.rsqrt(jnp.mean(vc * vc, axis=-1, keepdims=True) + EPS) * lng_ref[...]
             + lnb_ref[...]).astype(BF16)
        blocks = []
        for n in range(sub // SGU_BLOCK):
            cols = []
            for g in range(SGU_GROUPS):
                vb = v[n * SGU_BLOCK:(n + 1) * SGU_BLOCK, g * gdim:(g + 1) * gdim]
                cols.append(jnp.dot(wm[g], vb, preferred_element_type=F32) + bst_ref[:, g:g + 1])
            blocks.append(jnp.concatenate(cols, axis=1))
        s = blocks[0] if len(blocks) == 1 else jnp.concatenate(blocks, axis=0)
        y = jnp.dot(u * s.astype(BF16), wout_ref[...], preferred_element_type=F32)
        o_ref[rs, :] = x + gate_ref[...] * y


def _sgu_layer(x2d, seq, gain, shift, scale, gate, w_in, b_in, ln_g, ln_b, w_s, b_s, w_out):
    t, d = x2d.shape
    ffn = w_in.shape[1]
    half = ffn // 2
    tm = min(SGU_TOKENS, seq)
    sub = min(2 * SGU_BLOCK, tm)
    tpb = seq // tm
    row = lambda i: (i, 0)
    per_b = lambda i: (i // tpb, 0, 0)
    const = lambda i: (0, 0)
    once = pl.Buffered(1)
    return pl.pallas_call(
        functools.partial(_sgu_kernel, half=half, gdim=half // SGU_GROUPS, sub=sub),
        grid=(t // tm,),
        in_specs=[pl.BlockSpec((tm, d), row),
                  pl.BlockSpec((1, d), const),
                  pl.BlockSpec((None, 1, d), per_b), pl.BlockSpec((None, 1, d), per_b),
                  pl.BlockSpec((None, 1, d), per_b),
                  pl.BlockSpec((d, ffn), const, pipeline_mode=once), pl.BlockSpec((1, ffn), const),
                  pl.BlockSpec((1, half), const), pl.BlockSpec((1, half), const),
                  pl.BlockSpec(w_s.shape, lambda i: (0, 0, 0)),
                  pl.BlockSpec((SGU_BLOCK, SGU_GROUPS), const),
                  pl.BlockSpec((half, d), const, pipeline_mode=once)],
        out_specs=pl.BlockSpec((tm, d), row),
        out_shape=jax.ShapeDtypeStruct((t, d), F32),
        compiler_params=_params("parallel"),
        name="sgu_mixer",
    )(x2d, gain.reshape(1, d), scale, shift, gate, w_in.astype(BF16), b_in.reshape(1, ffn),
      ln_g.reshape(1, half), ln_b.reshape(1, half), w_s, b_s.T, w_out.astype(BF16))


def _weff_kernel(keys_ref, wq_ref, o_ref):
    o_ref[0] = lax.dot_general(keys_ref[0, 0], wq_ref[...], (((1,), (1,)), ((), ())),
                               preferred_element_type=F32, precision=lax.Precision.HIGHEST)


def _peer_score_weights(w_query, sub_keys):
    d = w_query.shape[0]
    nhp = PEER_HEADS * 2
    out = pl.pallas_call(
        _weff_kernel,
        grid=(nhp,),
        in_specs=[pl.BlockSpec((1, 1, PEER_NKEYS, PEER_HALF), lambda i: (i // 2, i % 2, 0, 0)),
                  pl.BlockSpec((d, PEER_HALF), lambda i: (0, i))],
        out_specs=pl.BlockSpec((1, PEER_NKEYS, d), lambda i: (i, 0, 0)),
        out_shape=jax.ShapeDtypeStruct((nhp, PEER_NKEYS, d), F32),
        compiler_params=_params("parallel"),
        name="peer_score_weights",
    )(sub_keys, w_query)
    out = out.reshape(PEER_HEADS, 2, PEER_NKEYS, d).transpose(1, 2, 0, 3)
    return out.reshape(2 * PEER_NKEYS * PEER_HEADS, d).astype(BF16)


def _dup_bf16(x):
    b = pltpu.bitcast(x, jnp.uint32)
    hi = (b + jnp.uint32(0x7FFF) + ((b >> 16) & jnp.uint32(1))) >> 16
    return hi | (hi << 16)


def _lex_first(vx, ix, vy, iy):
    return (vx > vy) | ((vx == vy) & (ix < iy))


def _ce(x, y):
    f = _lex_first(x[0], x[1], y[0], y[1])
    return ((jnp.where(f, x[0], y[0]), jnp.where(f, x[1], y[1])),
            (jnp.where(f, y[0], x[0]), jnp.where(f, y[1], x[1])))


def _bitonic_merge(z):
    n = len(z)
    j = n // 2
    while j >= 1:
        for i in range(n):
            l = i ^ j
            if l > i:
                z[i], z[l] = _ce(z[i], z[l])
        j //= 2
    return z


def _sort16(z):
    n = len(z)
    k = 2
    while k <= n:
        j = k // 2
        while j >= 1:
            for i in range(n):
                l = i ^ j
                if l > i:
                    if (i & k) == 0:
                        z[i], z[l] = _ce(z[i], z[l])
                    else:
                        z[l], z[i] = _ce(z[i], z[l])
            j //= 2
        k *= 2
    return z


def _top16_pairs(load, nkeys):
    k = PEER_TOPK
    top = None
    for g in range(nkeys // k):
        grp = _sort16([load(g * k + i) for i in range(k)])
        if top is None:
            top = grp
        else:
            z = []
            for i in range(k):
                a, b = top[i], grp[k - 1 - i]
                f = _lex_first(a[0], a[1], b[0], b[1])
                z.append((jnp.where(f, a[0], b[0]), jnp.where(f, a[1], b[1])))
            top = _bitonic_merge(z)
    return top


def _max_merge(z):
    n = len(z)
    j = n // 2
    while j >= 1:
        for i in range(n):
            l = i ^ j
            if l > i:
                z[i], z[l] = jnp.maximum(z[i], z[l]), jnp.minimum(z[i], z[l])
        j //= 2
    return z


def _sort16_vals(z):
    n = len(z)
    k = 2
    while k <= n:
        j = k // 2
        while j >= 1:
            for i in range(n):
                l = i ^ j
                if l > i:
                    hi, lo = jnp.maximum(z[i], z[l]), jnp.minimum(z[i], z[l])
                    z[i], z[l] = (hi, lo) if (i & k) == 0 else (lo, hi)
            j //= 2
        k *= 2
    return z


def _top16_vals(load, nkeys):
    k = PEER_TOPK
    top = None
    for g in range(nkeys // k):
        grp = _sort16_vals([load(g * k + i) for i in range(k)])
        top = grp if top is None else _max_merge([jnp.maximum(top[i], grp[k - 1 - i]) for i in range(k)])
    return top


def _route_kernel(x_ref, gain_ref, sc_ref, sh_ref, weff_ref,
                  ht_ref, n_ref, e1_ref, r2_ref, e2_ref, s_ref, r2s_ref, e2s_ref, m_ref, mode_ref):
    tm = x_ref.shape[0]
    k = PEER_TOPK
    nk = PEER_NKEYS
    hp = PEER_HEADS
    h2 = _norm_mod(x_ref[...], gain_ref[...], sc_ref[...], sh_ref[...])
    h2t = h2.T.astype(BF16)
    ht_ref[...] = h2t
    s_ref[...] = jnp.dot(weff_ref[...], h2t, preferred_element_type=F32)

    def group(gi, carry):
        lane = pl.ds(pl.multiple_of(gi * LANES, LANES), LANES)

        def score(p, key):
            r0 = (p * nk + key) * hp
            return s_ref[r0:r0 + hp, lane]

        v = [_top16_vals(functools.partial(score, p), nk) for p in range(2)]
        ties = jnp.zeros((hp, LANES), F32)
        for p in range(2):
            for b in range(k - 1):
                ties = ties + jnp.where(v[p][b] == v[p][b + 1], 1.0, 0.0)
            ge = jnp.zeros((hp, LANES), F32)
            for key in range(nk):
                ge = ge + jnp.where(score(p, key) >= v[p][k - 1], 1.0, 0.0)
            ties = ties + jnp.where(ge > float(k), 1.0, 0.0)
        for p in range(2):
            for a in range(k):
                m_ref[(p * k + a) * hp:(p * k + a + 1) * hp, :] = v[p][a]
        mode_ref[...] = jnp.zeros((hp, LANES), F32)

        @pl.when(jnp.max(ties) > 0.0)
        def _():
            for p in range(2):
                top = _top16_pairs(
                    lambda key: (score(p, key), jnp.full((hp, LANES), float(key), F32)), nk)
                for a in range(k):
                    m_ref[(p * k + a) * hp:(p * k + a + 1) * hp, :] = top[a][1]
            mode_ref[...] = jnp.ones((hp, LANES), F32)

        v1, v2 = v

        ln = [k // (a + 1) for a in range(k)]
        c = [[v1[a] + v2[b] for b in range(ln[a])] for a in range(k)]
        top = list(c[0])
        for a in range(1, k):
            for i in range(k - ln[a], k):
                top[i] = jnp.maximum(top[i], c[a][k - 1 - i])
            top = _max_merge(top)
        tau = top[k - 1]
        cnt_gt = []
        cnt_eq = []
        for a in range(k):
            gt = sum(jnp.where(c[a][b] > tau, 1.0, 0.0) for b in range(ln[a]))
            ge = sum(jnp.where(c[a][b] >= tau, 1.0, 0.0) for b in range(ln[a]))
            cnt_gt.append(gt)
            cnt_eq.append(ge - gt)
        rem = float(k) - sum(cnt_gt)
        n_sel = []
        for a in range(k):
            n_sel.append(cnt_gt[a] + jnp.clip(rem, 0.0, cnt_eq[a]))
            rem = rem - cnt_eq[a]
        m = c[0][0]
        z = 0.0
        for a in range(k):
            for b in range(ln[a]):
                z = z + jnp.where(n_sel[a] > float(b), jnp.exp(c[a][b] - m), 0.0)
        inv_z = 1.0 / z

        by_index = mode_ref[...] > 0.5

        def first_key_body(key, carry2):
            rows = pl.ds(pl.multiple_of(key * hp, hp), hp)
            s1 = s_ref[rows, lane]
            probe = jnp.where(by_index, jnp.asarray(key).astype(F32), s1)
            nacc = jnp.zeros((hp, LANES), F32)
            for a in range(k):
                nacc = jnp.where(m_ref[a * hp:(a + 1) * hp, :] == probe, n_sel[a], nacc)
            n_ref[gi, rows, :] = _dup_bf16(nacc)
            e1_ref[gi, rows, :] = _dup_bf16(jnp.exp(s1 - v1[0]) * inv_z)
            return carry2

        def second_key_body(key, carry2):
            rows = pl.ds(pl.multiple_of(key * hp, hp), hp)
            s2 = s_ref[pl.ds(pl.multiple_of(nk * hp + key * hp, hp), hp), lane]
            probe = jnp.where(by_index, jnp.asarray(key).astype(F32), s2)
            racc = jnp.full((hp, LANES), float(k), F32)
            for b in range(k):
                racc = jnp.where(m_ref[(k + b) * hp:(k + b + 1) * hp, :] == probe, float(b), racc)
            r2s_ref[rows, :] = racc
            e2s_ref[rows, :] = jnp.exp(s2 - v2[0])
            return carry2

        lax.fori_loop(0, nk, first_key_body, 0, unroll=8)
        lax.fori_loop(0, nk, second_key_body, 0, unroll=8)
        for h in range(hp):
            r2_ref[h * nk:(h + 1) * nk, lane] = r2s_ref[pl.ds(h, nk, stride=hp), :].astype(BF16)
            e2_ref[h * nk:(h + 1) * nk, lane] = e2s_ref[pl.ds(h, nk, stride=hp), :].astype(BF16)
        return carry

    lax.fori_loop(0, tm // LANES, group, 0)


def _peer_route(x2d, seq, gain, shift, scale, weff):
    t, d = x2d.shape
    tm = min(ROUTE_TOKENS, seq)
    tpb = seq // tm
    nrow = PEER_NKEYS * PEER_HEADS
    row = lambda i: (i, 0)
    col = lambda i: (0, i)
    per_b = lambda i: (i // tpb, 0, 0)
    const = lambda i: (0, 0)
    dup = jax.ShapeDtypeStruct((t // LANES, nrow, LANES), jnp.uint32)
    dup_spec = pl.BlockSpec((tm // LANES, nrow, LANES), lambda i: (i, 0, 0))
    tab = jax.ShapeDtypeStruct((nrow, t), BF16)
    return pl.pallas_call(
        _route_kernel,
        grid=(t // tm,),
        in_specs=[pl.BlockSpec((tm, d), row), pl.BlockSpec((1, d), const),
                  pl.BlockSpec((None, 1, d), per_b), pl.BlockSpec((None, 1, d), per_b),
                  pl.BlockSpec(weff.shape, const)],
        out_specs=[pl.BlockSpec((d, tm), col), dup_spec, dup_spec] + [pl.BlockSpec((nrow, tm), col)] * 2,
        out_shape=[jax.ShapeDtypeStruct((d, t), BF16), dup, dup, tab, tab],
        scratch_shapes=[pltpu.VMEM((2 * nrow, tm), F32), pltpu.VMEM((nrow, LANES), F32),
                        pltpu.VMEM((nrow, LANES), F32),
                        pltpu.VMEM((2 * PEER_TOPK * PEER_HEADS, LANES), F32),
                        pltpu.VMEM((PEER_HEADS, LANES), F32)],
        compiler_params=_params("parallel"),
        name="peer_route",
    )(x2d, gain.reshape(1, d), scale, shift, weff)


def _experts_kernel(ht_ref, u_ref, v_ref, r2_ref, e2_ref, n_ref, e1_ref, x_ref, gate_ref, nf_ref,
                    o_ref, acc_ref, act_ref, *, final):
    j = pl.program_id(1)
    te = u_ref.shape[0]
    tm = ht_ref.shape[1]
    nk = PEER_NKEYS
    hp = PEER_HEADS
    chunk = 2 * nk
    wg = 2
    zero = jnp.zeros((), BF16)

    @pl.when(j == 0)
    def _():
        acc_ref[...] = jnp.zeros_like(acc_ref)

    def bcast_row(ref, r, lg):
        row = jnp.concatenate([ref[lg * wg + g, r:r + 1, :] for g in range(wg)], axis=1)
        return pltpu.bitcast(jnp.broadcast_to(row, (nk // 2, wg * LANES)), BF16)

    for c in range(te // chunk):
        hid = jnp.dot(u_ref[c * chunk:(c + 1) * chunk, :], ht_ref[...], preferred_element_type=F32)
        for lg in range(tm // (wg * LANES)):
            lanes = slice(lg * wg * LANES, (lg + 1) * wg * LANES)
            gacc = [None, None]
            for h in range(hp):
                r2 = r2_ref[h * nk:(h + 1) * nk, lanes]
                e2 = e2_ref[h * nk:(h + 1) * nk, lanes]
                for s2 in range(2):
                    r = (2 * c + s2) * hp + h
                    sel = jnp.where(r2 < bcast_row(n_ref, r, lg), e2, zero) * bcast_row(e1_ref, r, lg)
                    gacc[s2] = sel if gacc[s2] is None else gacc[s2] + sel
            for s2 in range(2):
                act_ref[c * chunk + s2 * nk: c * chunk + (s2 + 1) * nk, lanes] = (
                    _gelu(hid[s2 * nk:(s2 + 1) * nk, lanes].astype(BF16)) * gacc[s2])
    acc_ref[...] += lax.dot_general(v_ref[...], act_ref[...], (((0,), (0,)), ((), ())),
                                    preferred_element_type=F32)

    @pl.when(j == pl.num_programs(1) - 1)
    def _():
        y = x_ref[...] + gate_ref[...] * acc_ref[...].T
        if final:
            y = _rms(y, nf_ref[...])
        o_ref[...] = y


def _peer_experts(x2d, seq, gate, ht, n_tab, e1_tab, r2_tab, e2_tab, u_all, v_all, layer, norm_final, final):
    t, d = x2d.shape
    ne = u_all.shape[1]
    tm = min(EXPERT_TOKENS, seq)
    te = EXPERT_TILE
    tpb = seq // tm
    nrow = PEER_NKEYS * PEER_HEADS
    srow = (te // PEER_NKEYS) * PEER_HEADS
    return pl.pallas_call(
        functools.partial(_experts_kernel, final=final),
        grid=(t // tm, ne // te),
        in_specs=[pl.BlockSpec((d, tm), lambda i, j: (0, i)),
                  pl.BlockSpec((None, te, d), lambda i, j: (layer, j, 0)),
                  pl.BlockSpec((None, te, d), lambda i, j: (layer, j, 0)),
                  pl.BlockSpec((nrow, tm), lambda i, j: (0, i)),
                  pl.BlockSpec((nrow, tm), lambda i, j: (0, i)),
                  pl.BlockSpec((tm // LANES, srow, LANES), lambda i, j: (i, j, 0)),
                  pl.BlockSpec((tm // LANES, srow, LANES), lambda i, j: (i, j, 0)),
                  pl.BlockSpec((tm, d), lambda i, j: (i, 0)),
                  pl.BlockSpec((None, 1, d), lambda i, j: (i // tpb, 0, 0)),
                  pl.BlockSpec((1, d), lambda i, j: (0, 0))],
        out_specs=pl.BlockSpec((tm, d), lambda i, j: (i, 0)),
        out_shape=jax.ShapeDtypeStruct((t, d), F32),
        scratch_shapes=[pltpu.VMEM((d, tm), F32), pltpu.VMEM((te, tm), BF16)],
        compiler_params=_params("parallel", "arbitrary"),
        name="peer_experts",
    )(ht, u_all, v_all, r2_tab, e2_tab, n_tab, e1_tab, x2d, gate, norm_final.reshape(1, d))


def _peer_layer(x2d, seq, gain, shift, scale, gate, w_query, sub_keys, u_all, v_all, layer, norm_final, final):
    weff = _peer_score_weights(w_query, sub_keys)
    ht, n_tab, e1_tab, r2_tab, e2_tab = _peer_route(x2d, seq, gain, shift, scale, weff)
    return _peer_experts(x2d, seq, gate, ht, n_tab, e1_tab, r2_tab, e2_tab,
                         u_all, v_all, layer, norm_final, final)


def kernel(x, c, positions, norm_mix, norm_ffn, ada_w, ada_b, ret_w_in, ret_w_out, sgu_w_in, sgu_b_in,
           sgu_ln_g, sgu_ln_b, sgu_w_s, sgu_b_s, sgu_w_out, peer_w_query, peer_sub_keys, peer_u, peer_v,
           norm_final):
    bsz, seq, d = x.shape
    depth = ada_w.shape[0]
    mods = _mods(c, ada_w, ada_b)
    xt = x.reshape(bsz * seq, d)
    u_all = peer_u.astype(BF16)
    v_all = peer_v.astype(BF16)
    for layer in range(depth):
        sh1, sc1, g1, sh2, sc2, g2 = (mods[layer, :, i] for i in range(6))
        j = layer // 2
        if layer % 2 == 0:
            xt = _retention_layer(xt, positions, bsz, seq, norm_mix[layer], sh1, sc1, g1,
                                  ret_w_in[j], ret_w_out[j])
        else:
            xt = _sgu_layer(xt, seq, norm_mix[layer], sh1, sc1, g1, sgu_w_in[j], sgu_b_in[j],
                            sgu_ln_g[j], sgu_ln_b[j], sgu_w_s[j], sgu_b_s[j], sgu_w_out[j])
        xt = _peer_layer(xt, seq, norm_ffn[layer], sh2, sc2, g2, peer_w_query[layer], peer_sub_keys[layer],
                         u_all, v_all, layer, norm_final, layer == depth - 1)
    return xt.reshape(bsz, seq, d)
```

```python
import functools
import math

import jax
import jax.numpy as jnp
from jax import lax
from jax.experimental import pallas as pl
from jax.experimental.pallas import tpu as pltpu

F32 = jnp.float32
BF16 = jnp.bfloat16

EPS = 1e-6
CHUNK = 64
RET_HEADS = 4
ROPE_BASE = 10000.0
SGU_BLOCK = 128
SGU_GROUPS = 8
PEER_HEADS = 8
PEER_NKEYS = 128
PEER_HALF = 128
PEER_TOPK = 16

LANES = 128
SUBLANES = 8
VMEM_LIMIT_BYTES = 56 * 1024 * 1024

RET_BLOCK = 256
INPROJ_TOKENS = 512
SGU_TOKENS = 512
ROUTE_TOKENS = 512
EXPERT_TOKENS = 1024
EXPERT_TILE = 2048
ADALN_COLS = 1536


def _params(*sem):
    return pltpu.CompilerParams(dimension_semantics=sem, vmem_limit_bytes=VMEM_LIMIT_BYTES)


_GELU_A1 = 2.0 * math.sqrt(2.0 / math.pi) * math.log2(math.e)


def _gelu(x):
    z = (x * x * (-_GELU_A1 * 0.044715) + (-_GELU_A1)) * x
    return x * (1.0 / (1.0 + jnp.exp2(z)))


def _silu(x):
    return x * (1.0 / (1.0 + jnp.exp(-x)))


def _rms(x, gain):
    return x * lax.rsqrt(jnp.mean(x * x, axis=-1, keepdims=True) + EPS) * gain


def _norm_mod(x, gain, scale, shift):
    return _rms(x, gain) * (1.0 + scale) + shift


def _mods_kernel(c_ref, w_ref, b_ref, o_ref):
    o_ref[...] = jnp.dot(_silu(c_ref[...]), w_ref[...], preferred_element_type=F32,
                         precision=lax.Precision.HIGHEST) + b_ref[...]


def _mods(c, ada_w, ada_b):
    depth, d, n = ada_w.shape
    bsz = c.shape[0]
    cp = jnp.zeros((SUBLANES, d), F32).at[:bsz].set(c)
    tn = ADALN_COLS
    out = pl.pallas_call(
        _mods_kernel,
        grid=(depth, n // tn),
        in_specs=[pl.BlockSpec((SUBLANES, d), lambda l, j: (0, 0)),
                  pl.BlockSpec((None, d, tn), lambda l, j: (l, 0, j)),
                  pl.BlockSpec((None, 1, tn), lambda l, j: (l, 0, j))],
        out_specs=pl.BlockSpec((None, SUBLANES, tn), lambda l, j: (l, 0, j)),
        out_shape=jax.ShapeDtypeStruct((depth, SUBLANES, n), F32),
        compiler_params=_params("parallel", "parallel"),
        name="adaln_mods",
    )(cp, ada_w, ada_b.reshape(depth, 1, n))
    return out[:, :bsz].reshape(depth, bsz, 6, 1, d)


def _inproj_kernel(x_ref, pos_ref, invf_ref, gain_ref, sc_ref, sh_ref, w_ref,
                   q_ref, k_ref, v_ref, g_ref, *, d, dk):
    h = _norm_mod(x_ref[...], gain_ref[...], sc_ref[...], sh_ref[...]).astype(BF16)
    ang = pos_ref[...].astype(F32) * invf_ref[...]
    cos = jnp.cos(ang)
    sin = jnp.sin(ang)
    half = dk // 2
    for base, o_ref, scale in ((0, q_ref, None), (d, k_ref, dk ** -0.5)):
        for hd in range(d // dk):
            y = jnp.dot(h, w_ref[:, base + hd * dk: base + (hd + 1) * dk], preferred_element_type=F32)
            x1 = y[:, :half]
            x2 = y[:, half:]
            o1 = x1 * cos - x2 * sin
            o2 = x1 * sin + x2 * cos
            if scale is not None:
                o1 = o1 * scale
                o2 = o2 * scale
            o_ref[:, hd * dk: hd * dk + half] = o1.astype(BF16)
            o_ref[:, hd * dk + half: (hd + 1) * dk] = o2.astype(BF16)
    vw = v_ref.shape[1]
    cw = 512
    for base, o_ref in ((2 * d, v_ref), (2 * d + vw, g_ref)):
        for j in range(vw // cw):
            o_ref[:, j * cw:(j + 1) * cw] = jnp.dot(
                h, w_ref[:, base + j * cw: base + (j + 1) * cw], preferred_element_type=F32).astype(BF16)


def _ret_kernel(q_ref, k_ref, v_ref, g_ref, dm_ref, xi_ref, zt_ref, wout_ref, x_ref, gate_ref,
                o_ref, s_ref, *, dk, dv, gl):
    @pl.when(pl.program_id(1) == 0)
    def _():
        s_ref[...] = jnp.zeros_like(s_ref)

    proj = None
    for h in range(len(gl)):
        q = q_ref[:, h * dk:(h + 1) * dk]
        k = k_ref[:, h * dk:(h + 1) * dk]
        v = v_ref[:, h * dv:(h + 1) * dv]
        sc = lax.dot_general(q, k, (((1,), (1,)), ((), ())), preferred_element_type=F32) * dm_ref[h]
        intra = jnp.dot(sc.astype(BF16), v, preferred_element_type=F32)
        st = s_ref[h]
        qx = (q.astype(F32) * xi_ref[h]).astype(BF16)
        cross = jnp.dot(qx, st.astype(BF16), preferred_element_type=F32)
        kz = (k.astype(F32) * zt_ref[h]).astype(BF16)
        s_ref[h] = st * gl[h] + lax.dot_general(kz, v, (((0,), (0,)), ((), ())), preferred_element_type=F32)
        y = intra + cross
        yn = y * lax.rsqrt(jnp.mean(y * y, axis=-1, keepdims=True) + EPS)
        yg = (_silu(g_ref[:, h * dv:(h + 1) * dv].astype(F32)) * yn).astype(BF16)
        part = jnp.dot(yg, wout_ref[h * dv:(h + 1) * dv, :], preferred_element_type=F32)
        proj = part if proj is None else proj + part
    o_ref[...] = x_ref[...] + gate_ref[...] * proj


def _retention_tables(dk):
    lb = RET_BLOCK
    log_gamma = jnp.log(1.0 - 2.0 ** (-5.0 - jnp.arange(RET_HEADS, dtype=F32)))[:, None, None]
    idx = jnp.arange(lb, dtype=F32)
    diff = idx[:, None] - idx[None, :]
    ci = jnp.arange(lb)[:, None] // CHUNK
    cj = jnp.arange(lb)[None, :] // CHUNK
    dist = jnp.where(ci == cj, jnp.abs(diff), diff)
    dm = jnp.where((cj <= ci)[None], jnp.exp(log_gamma * dist[None]), 0.0)
    xi = jnp.broadcast_to(jnp.exp(log_gamma * (idx[None, :, None] + 1.0)), (RET_HEADS, lb, dk))
    zt = jnp.broadcast_to(jnp.exp(log_gamma * (lb - 1.0 - idx[None, :, None])), (RET_HEADS, lb, dk))
    gl = tuple(float((1.0 - 2.0 ** (-5.0 - h)) ** lb) for h in range(RET_HEADS))
    return dm, xi, zt, gl


def _retention_layer(x2d, positions, bsz, seq, gain, shift, scale, gate, w_in, w_out):
    t, d = x2d.shape
    dk = d // RET_HEADS
    dv = 2 * dk
    vw = RET_HEADS * dv
    tm = min(INPROJ_TOKENS, seq)
    tpb = seq // tm
    half = dk // 2
    inv_freq = (1.0 / (ROPE_BASE ** (jnp.arange(half, dtype=F32) / half))).reshape(1, half)
    row = lambda i: (i, 0)
    per_b = lambda i: (i // tpb, 0, 0)
    const = lambda i: (0, 0)
    q, k, v, g = pl.pallas_call(
        functools.partial(_inproj_kernel, d=d, dk=dk),
        grid=(t // tm,),
        in_specs=[pl.BlockSpec((tm, d), row),
                  pl.BlockSpec((tm, 1), row),
                  pl.BlockSpec((1, half), const),
                  pl.BlockSpec((1, d), const),
                  pl.BlockSpec((None, 1, d), per_b),
                  pl.BlockSpec((None, 1, d), per_b),
                  pl.BlockSpec(w_in.shape, const)],
        out_specs=[pl.BlockSpec((tm, d), row), pl.BlockSpec((tm, d), row),
                   pl.BlockSpec((tm, vw), row), pl.BlockSpec((tm, vw), row)],
        out_shape=[jax.ShapeDtypeStruct((t, d), BF16), jax.ShapeDtypeStruct((t, d), BF16),
                   jax.ShapeDtypeStruct((t, vw), BF16), jax.ShapeDtypeStruct((t, vw), BF16)],
        compiler_params=_params("parallel"),
        name="ret_inproj_rope",
    )(x2d, positions.reshape(t, 1), inv_freq, gain.reshape(1, d), scale, shift, w_in.astype(BF16))

    dm, xi, zt, gl = _retention_tables(dk)
    lb = RET_BLOCK
    nblk = seq // lb
    blk = lambda b, i: (b * nblk + i, 0)
    full3 = lambda b, i: (0, 0, 0)
    return pl.pallas_call(
        functools.partial(_ret_kernel, dk=dk, dv=dv, gl=gl),
        grid=(bsz, nblk),
        in_specs=[pl.BlockSpec((lb, d), blk), pl.BlockSpec((lb, d), blk),
                  pl.BlockSpec((lb, vw), blk), pl.BlockSpec((lb, vw), blk),
                  pl.BlockSpec(dm.shape, full3), pl.BlockSpec(xi.shape, full3), pl.BlockSpec(zt.shape, full3),
                  pl.BlockSpec((vw, d), lambda b, i: (0, 0)),
                  pl.BlockSpec((lb, d), blk),
                  pl.BlockSpec((None, 1, d), lambda b, i: (b, 0, 0))],
        out_specs=pl.BlockSpec((lb, d), blk),
        out_shape=jax.ShapeDtypeStruct((t, d), F32),
        scratch_shapes=[pltpu.VMEM((RET_HEADS, dk, dv), F32)],
        compiler_params=_params("parallel", "arbitrary"),
        name="ret_scan_outproj",
    )(q, k, v, g, dm, xi, zt, w_out.astype(BF16), x2d, gate)


def _sgu_kernel(x_ref, gain_ref, sc_ref, sh_ref, gate_ref, win_ref, bin_ref, lng_ref, lnb_ref,
                ws_ref, bst_ref, wout_ref, o_ref, *, half, gdim, sub):
    pi = lax.broadcasted_iota(jnp.int32, (SGU_BLOCK, SGU_BLOCK), 0) // CHUNK
    pj = lax.broadcasted_iota(jnp.int32, (SGU_BLOCK, SGU_BLOCK), 1) // CHUNK
    keep = pj <= pi
    wm = [jnp.where(keep, ws_ref[g], 0.0).astype(BF16) for g in range(SGU_GROUPS)]
    for r in range(x_ref.shape[0] // sub):
        rs = slice(r * sub, (r + 1) * sub)
        x = x_ref[rs, :]
        h = _norm_mod(x, gain_ref[...], sc_ref[...], sh_ref[...]).astype(BF16)
        u = _gelu((jnp.dot(h, win_ref[:, :half], preferred_element_type=F32) + bin_ref[:, :half]).astype(BF16))
        v = _gelu(jnp.dot(h, win_ref[:, half:], preferred_element_type=F32) + bin_ref[:, half:])
        mu = jnp.mean(v, axis=-1, keepdims=True)
        vc = v - mu
        v = (vc * lax.rsqrt(jnp.mean(vc * vc, axis=-1, keepdims=True) + EPS) * lng_ref[...]
             + lnb_ref[...]).astype(BF16)
        blocks = []
        for n in range(sub // SGU_BLOCK):
            cols = []
            for g in range(SGU_GROUPS):
                vb = v[n * SGU_BLOCK:(n + 1) * SGU_BLOCK, g * gdim:(g + 1) * gdim]
                cols.append(jnp.dot(wm[g], vb, preferred_element_type=F32) + bst_ref[:, g:g + 1])
            blocks.append(jnp.concatenate(cols, axis=1))
        s = blocks[0] if len(blocks) == 1 else jnp.concatenate(blocks, axis=0)
        y = jnp.dot(u * s.astype(BF16), wout_ref[...], preferred_element_type=F32)
        o_ref[rs, :] = x + gate_ref[...] * y


def _sgu_layer(x2d, seq, gain, shift, scale, gate, w_in, b_in, ln_g, ln_b, w_s, b_s, w_out):
    t, d = x2d.shape
    ffn = w_in.shape[1]
    half = ffn // 2
    tm = min(SGU_TOKENS, seq)
    sub = min(2 * SGU_BLOCK, tm)
    tpb = seq // tm
    row = lambda i: (i, 0)
    per_b = lambda i: (i // tpb, 0, 0)
    const = lambda i: (0, 0)
    once = pl.Buffered(1)
    return pl.pallas_call(
        functools.partial(_sgu_kernel, half=half, gdim=half // SGU_GROUPS, sub=sub),
        grid=(t // tm,),
        in_specs=[pl.BlockSpec((tm, d), row),
                  pl.BlockSpec((1, d), const),
                  pl.BlockSpec((None, 1, d), per_b), pl.BlockSpec((None, 1, d), per_b),
                  pl.BlockSpec((None, 1, d), per_b),
                  pl.BlockSpec((d, ffn), const, pipeline_mode=once), pl.BlockSpec((1, ffn), const),
                  pl.BlockSpec((1, half), const), pl.BlockSpec((1, half), const),
                  pl.BlockSpec(w_s.shape, lambda i: (0, 0, 0)),
                  pl.BlockSpec((SGU_BLOCK, SGU_GROUPS), const),
                  pl.BlockSpec((half, d), const, pipeline_mode=once)],
        out_specs=pl.BlockSpec((tm, d), row),
        out_shape=jax.ShapeDtypeStruct((t, d), F32),
        compiler_params=_params("parallel"),
        name="sgu_mixer",
    )(x2d, gain.reshape(1, d), scale, shift, gate, w_in.astype(BF16), b_in.reshape(1, ffn),
      ln_g.reshape(1, half), ln_b.reshape(1, half), w_s, b_s.T, w_out.astype(BF16))


def _weff_kernel(keys_ref, wq_ref, o_ref):
    o_ref[0] = lax.dot_general(keys_ref[0, 0], wq_ref[...], (((1,), (1,)), ((), ())),
                               preferred_element_type=F32, precision=lax.Precision.HIGHEST)


def _peer_score_weights(w_query, sub_keys):
    d = w_query.shape[0]
    nhp = PEER_HEADS * 2
    out = pl.pallas_call(
        _weff_kernel,
        grid=(nhp,),
        in_specs=[pl.BlockSpec((1, 1, PEER_NKEYS, PEER_HALF), lambda i: (i // 2, i % 2, 0, 0)),
                  pl.BlockSpec((d, PEER_HALF), lambda i: (0, i))],
        out_specs=pl.BlockSpec((1, PEER_NKEYS, d), lambda i: (i, 0, 0)),
        out_shape=jax.ShapeDtypeStruct((nhp, PEER_NKEYS, d), F32),
        compiler_params=_params("parallel"),
        name="peer_score_weights",
    )(sub_keys, w_query)
    out = out.reshape(PEER_HEADS, 2, PEER_NKEYS, d).transpose(1, 2, 0, 3)
    return out.reshape(2 * PEER_NKEYS * PEER_HEADS, d).astype(BF16)


def _dup_bf16(x):
    b = pltpu.bitcast(x, jnp.uint32)
    hi = (b + jnp.uint32(0x7FFF) + ((b >> 16) & jnp.uint32(1))) >> 16
    return hi | (hi << 16)


def _lex_first(vx, ix, vy, iy):
    return (vx > vy) | ((vx == vy) & (ix < iy))


def _ce(x, y):
    f = _lex_first(x[0], x[1], y[0], y[1])
    return ((jnp.where(f, x[0], y[0]), jnp.where(f, x[1], y[1])),
            (jnp.where(f, y[0], x[0]), jnp.where(f, y[1], x[1])))


def _bitonic_merge(z):
    n = len(z)
    j = n // 2
    while j >= 1:
        for i in range(n):
            l = i ^ j
            if l > i:
                z[i], z[l] = _ce(z[i], z[l])
        j //= 2
    return z


def _sort16(z):
    n = len(z)
    k = 2
    while k <= n:
        j = k // 2
        while j >= 1:
            for i in range(n):
                l = i ^ j
                if l > i:
                    if (i & k) == 0:
                        z[i], z[l] = _ce(z[i], z[l])
                    else:
                        z[l], z[i] = _ce(z[i], z[l])
            j //= 2
        k *= 2
    return z


def _top16_pairs(load, nkeys):
    k = PEER_TOPK
    top = None
    for g in range(nkeys // k):
        grp = _sort16([load(g * k + i) for i in range(k)])
        if top is None:
            top = grp
        else:
            z = []
            for i in range(k):
                a, b = top[i], grp[k - 1 - i]
                f = _lex_first(a[0], a[1], b[0], b[1])
                z.append((jnp.where(f, a[0], b[0]), jnp.where(f, a[1], b[1])))
            top = _bitonic_merge(z)
    return top


def _max_merge(z):
    n = len(z)
    j = n // 2
    while j >= 1:
        for i in range(n):
            l = i ^ j
            if l > i:
                z[i], z[l] = jnp.maximum(z[i], z[l]), jnp.minimum(z[i], z[l])
        j //= 2
    return z


def _sort16_vals(z):
    n = len(z)
    k = 2
    while k <= n:
        j = k // 2
        while j >= 1:
            for i in range(n):
                l = i ^ j
                if l > i:
                    hi, lo = jnp.maximum(z[i], z[l]), jnp.minimum(z[i], z[l])
                    z[i], z[l] = (hi, lo) if (i & k) == 0 else (lo, hi)
            j //= 2
        k *= 2
    return z


def _top16_vals(load, nkeys):
    k = PEER_TOPK
    top = None
    for g in range(nkeys // k):
        grp = _sort16_vals([load(g * k + i) for i in range(k)])
        top = grp if top is None else _max_merge([jnp.maximum(top[i], grp[k - 1 - i]) for i in range(k)])
    return top


def _route_kernel(x_ref, gain_ref, sc_ref, sh_ref, weff_ref,
                  ht_ref, n_ref, e1_ref, r2_ref, e2_ref, s_ref, r2s_ref, e2s_ref, m_ref, mode_ref):
    tm = x_ref.shape[0]
    k = PEER_TOPK
    nk = PEER_NKEYS
    hp = PEER_HEADS
    h2 = _norm_mod(x_ref[...], gain_ref[...], sc_ref[...], sh_ref[...])
    h2t = h2.T.astype(BF16)
    ht_ref[...] = h2t
    s_ref[...] = jnp.dot(weff_ref[...], h2t, preferred_element_type=F32)

    def group(gi, carry):
        lane = pl.ds(pl.multiple_of(gi * LANES, LANES), LANES)

        def score(p, key):
            r0 = (p * nk + key) * hp
            return s_ref[r0:r0 + hp, lane]

        v = [_top16_vals(functools.partial(score, p), nk) for p in range(2)]
        ties = jnp.zeros((hp, LANES), F32)
        for p in range(2):
            for b in range(k - 1):
                ties = ties + jnp.where(v[p][b] == v[p][b + 1], 1.0, 0.0)
            ge = jnp.zeros((hp, LANES), F32)
            for key in range(nk):
                ge = ge + jnp.where(score(p, key) >= v[p][k - 1], 1.0, 0.0)
            ties = ties + jnp.where(ge > float(k), 1.0, 0.0)
        for p in range(2):
            for a in range(k):
                m_ref[(p * k + a) * hp:(p * k + a + 1) * hp, :] = v[p][a]
        mode_ref[...] = jnp.zeros((hp, LANES), F32)

        @pl.when(jnp.max(ties) > 0.0)
        def _():
            for p in range(2):
                top = _top16_pairs(
                    lambda key: (score(p, key), jnp.full((hp, LANES), float(key), F32)), nk)
                for a in range(k):
                    m_ref[(p * k + a) * hp:(p * k + a + 1) * hp, :] = top[a][1]
            mode_ref[...] = jnp.ones((hp, LANES), F32)

        v1, v2 = v

        ln = [k // (a + 1) for a in range(k)]
        c = [[v1[a] + v2[b] for b in range(ln[a])] for a in range(k)]
        top = list(c[0])
        for a in range(1, k):
            for i in range(k - ln[a], k):
                top[i] = jnp.maximum(top[i], c[a][k - 1 - i])
            top = _max_merge(top)
        tau = top[k - 1]
        cnt_gt = []
        cnt_eq = []
        for a in range(k):
            gt = sum(jnp.where(c[a][b] > tau, 1.0, 0.0) for b in range(ln[a]))
            ge = sum(jnp.where(c[a][b] >= tau, 1.0, 0.0) for b in range(ln[a]))
            cnt_gt.append(gt)
            cnt_eq.append(ge - gt)
        rem = float(k) - sum(cnt_gt)
        n_sel = []
        for a in range(k):
            n_sel.append(cnt_gt[a] + jnp.clip(rem, 0.0, cnt_eq[a]))
            rem = rem - cnt_eq[a]
        m = c[0][0]
        z = 0.0
        for a in range(k):
            for b in range(ln[a]):
                z = z + jnp.where(n_sel[a] > float(b), jnp.exp(c[a][b] - m), 0.0)
        inv_z = 1.0 / z

        by_index = mode_ref[...] > 0.5

        def first_key_body(key, carry2):
            rows = pl.ds(pl.multiple_of(key * hp, hp), hp)
            s1 = s_ref[rows, lane]
            probe = jnp.where(by_index, jnp.asarray(key).astype(F32), s1)
            nacc = jnp.zeros((hp, LANES), F32)
            for a in range(k):
                nacc = jnp.where(m_ref[a * hp:(a + 1) * hp, :] == probe, n_sel[a], nacc)
            n_ref[gi, rows, :] = _dup_bf16(nacc)
            e1_ref[gi, rows, :] = _dup_bf16(jnp.exp(s1 - v1[0]) * inv_z)
            return carry2

        def second_key_body(key, carry2):
            rows = pl.ds(pl.multiple_of(key * hp, hp), hp)
            s2 = s_ref[pl.ds(pl.multiple_of(nk * hp + key * hp, hp), hp), lane]
            probe = jnp.where(by_index, jnp.asarray(key).astype(F32), s2)
            racc = jnp.full((hp, LANES), float(k), F32)
            for b in range(k):
                racc = jnp.where(m_ref[(k + b) * hp:(k + b + 1) * hp, :] == probe, float(b), racc)
            r2s_ref[rows, :] = racc
            e2s_ref[rows, :] = jnp.exp(s2 - v2[0])
            return carry2

        lax.fori_loop(0, nk, first_key_body, 0, unroll=8)
        lax.fori_loop(0, nk, second_key_body, 0, unroll=8)
        for h in range(hp):
            r2_ref[h * nk:(h + 1) * nk, lane] = r2s_ref[pl.ds(h, nk, stride=hp), :].astype(BF16)
            e2_ref[h * nk:(h + 1) * nk, lane] = e2s_ref[pl.ds(h, nk, stride=hp), :].astype(BF16)
        return carry

    lax.fori_loop(0, tm // LANES, group, 0)


def _peer_route(x2d, seq, gain, shift, scale, weff):
    t, d = x2d.shape
    tm = min(ROUTE_TOKENS, seq)
    tpb = seq // tm
    nrow = PEER_NKEYS * PEER_HEADS
    row = lambda i: (i, 0)
    col = lambda i: (0, i)
    per_b = lambda i: (i // tpb, 0, 0)
    const = lambda i: (0, 0)
    dup = jax.ShapeDtypeStruct((t // LANES, nrow, LANES), jnp.uint32)
    dup_spec = pl.BlockSpec((tm // LANES, nrow, LANES), lambda i: (i, 0, 0))
    tab = jax.ShapeDtypeStruct((nrow, t), BF16)
    return pl.pallas_call(
        _route_kernel,
        grid=(t // tm,),
        in_specs=[pl.BlockSpec((tm, d), row), pl.BlockSpec((1, d), const),
                  pl.BlockSpec((None, 1, d), per_b), pl.BlockSpec((None, 1, d), per_b),
                  pl.BlockSpec(weff.shape, const)],
        out_specs=[pl.BlockSpec((d, tm), col), dup_spec, dup_spec] + [pl.BlockSpec((nrow, tm), col)] * 2,
        out_shape=[jax.ShapeDtypeStruct((d, t), BF16), dup, dup, tab, tab],
        scratch_shapes=[pltpu.VMEM((2 * nrow, tm), F32), pltpu.VMEM((nrow, LANES), F32),
                        pltpu.VMEM((nrow, LANES), F32),
                        pltpu.VMEM((2 * PEER_TOPK * PEER_HEADS, LANES), F32),
                        pltpu.VMEM((PEER_HEADS, LANES), F32)],
        compiler_params=_params("parallel"),
        name="peer_route",
    )(x2d, gain.reshape(1, d), scale, shift, weff)


def _experts_kernel(ht_ref, u_ref, v_ref, r2_ref, e2_ref, n_ref, e1_ref, x_ref, gate_ref, nf_ref,
                    o_ref, acc_ref, act_ref, *, final):
    j = pl.program_id(1)
    te = u_ref.shape[0]
    tm = ht_ref.shape[1]
    nk = PEER_NKEYS
    hp = PEER_HEADS
    chunk = 2 * nk
    wg = 2
    zero = jnp.zeros((), BF16)

    @pl.when(j == 0)
    def _():
        acc_ref[...] = jnp.zeros_like(acc_ref)

    def bcast_row(ref, r, lg):
        row = jnp.concatenate([ref[lg * wg + g, r:r + 1, :] for g in range(wg)], axis=1)
        return pltpu.bitcast(jnp.broadcast_to(row, (nk // 2, wg * LANES)), BF16)

    for c in range(te // chunk):
        hid = jnp.dot(u_ref[c * chunk:(c + 1) * chunk, :], ht_ref[...], preferred_element_type=F32)
        for lg in range(tm // (wg * LANES)):
            lanes = slice(lg * wg * LANES, (lg + 1) * wg * LANES)
            gacc = [None, None]
            for h in range(hp):
                r2 = r2_ref[h * nk:(h + 1) * nk, lanes]
                e2 = e2_ref[h * nk:(h + 1) * nk, lanes]
                for s2 in range(2):
                    r = (2 * c + s2) * hp + h
                    sel = jnp.where(r2 < bcast_row(n_ref, r, lg), e2, zero) * bcast_row(e1_ref, r, lg)
                    gacc[s2] = sel if gacc[s2] is None else gacc[s2] + sel
            for s2 in range(2):
                act_ref[c * chunk + s2 * nk: c * chunk + (s2 + 1) * nk, lanes] = (
                    _gelu(hid[s2 * nk:(s2 + 1) * nk, lanes].astype(BF16)) * gacc[s2])
    acc_ref[...] += lax.dot_general(v_ref[...], act_ref[...], (((0,), (0,)), ((), ())),
                                    preferred_element_type=F32)

    @pl.when(j == pl.num_programs(1) - 1)
    def _():
        y = x_ref[...] + gate_ref[...] * acc_ref[...].T
        if final:
            y = _rms(y, nf_ref[...])
        o_ref[...] = y


def _peer_experts(x2d, seq, gate, ht, n_tab, e1_tab, r2_tab, e2_tab, u_all, v_all, layer, norm_final, final):
    t, d = x2d.shape
    ne = u_all.shape[1]
    tm = min(EXPERT_TOKENS, seq)
    te = EXPERT_TILE
    tpb = seq // tm
    nrow = PEER_NKEYS * PEER_HEADS
    srow = (te // PEER_NKEYS) * PEER_HEADS
    per_tile = pl.Buffered(1)
    return pl.pallas_call(
        functools.partial(_experts_kernel, final=final),
        grid=(t // tm, ne // te),
        in_specs=[pl.BlockSpec((d, tm), lambda i, j: (0, i), pipeline_mode=per_tile),
                  pl.BlockSpec((None, te, d), lambda i, j: (layer, j, 0)),
                  pl.BlockSpec((None, te, d), lambda i, j: (layer, j, 0)),
                  pl.BlockSpec((nrow, tm), lambda i, j: (0, i), pipeline_mode=per_tile),
                  pl.BlockSpec((nrow, tm), lambda i, j: (0, i), pipeline_mode=per_tile),
                  pl.BlockSpec((tm // LANES, srow, LANES), lambda i, j: (i, j, 0)),
                  pl.BlockSpec((tm // LANES, srow, LANES), lambda i, j: (i, j, 0)),
                  pl.BlockSpec((tm, d), lambda i, j: (i, 0), pipeline_mode=per_tile),
                  pl.BlockSpec((None, 1, d), lambda i, j: (i // tpb, 0, 0)),
                  pl.BlockSpec((1, d), lambda i, j: (0, 0))],
        out_specs=pl.BlockSpec((tm, d), lambda i, j: (i, 0)),
        out_shape=jax.ShapeDtypeStruct((t, d), F32),
        scratch_shapes=[pltpu.VMEM((d, tm), F32), pltpu.VMEM((te, tm), BF16)],
        compiler_params=_params("parallel", "arbitrary"),
        name="peer_experts",
    )(ht, u_all, v_all, r2_tab, e2_tab, n_tab, e1_tab, x2d, gate, norm_final.reshape(1, d))


def _peer_layer(x2d, seq, gain, shift, scale, gate, w_query, sub_keys, u_all, v_all, layer, norm_final, final):
    weff = _peer_score_weights(w_query, sub_keys)
    ht, n_tab, e1_tab, r2_tab, e2_tab = _peer_route(x2d, seq, gain, shift, scale, weff)
    return _peer_experts(x2d, seq, gate, ht, n_tab, e1_tab, r2_tab, e2_tab,
                         u_all, v_all, layer, norm_final, final)


def kernel(x, c, positions, norm_mix, norm_ffn, ada_w, ada_b, ret_w_in, ret_w_out, sgu_w_in, sgu_b_in,
           sgu_ln_g, sgu_ln_b, sgu_w_s, sgu_b_s, sgu_w_out, peer_w_query, peer_sub_keys, peer_u, peer_v,
           norm_final):
    bsz, seq, d = x.shape
    depth = ada_w.shape[0]
    mods = _mods(c, ada_w, ada_b)
    xt = x.reshape(bsz * seq, d)
    u_all = peer_u.astype(BF16)
    v_all = peer_v.astype(BF16)
    for layer in range(depth):
        sh1, sc1, g1, sh2, sc2, g2 = (mods[layer, :, i] for i in range(6))
        j = layer // 2
        if layer % 2 == 0:
            xt = _retention_layer(xt, positions, bsz, seq, norm_mix[layer], sh1, sc1, g1,
                                  ret_w_in[j], ret_w_out[j])
        else:
            xt = _sgu_layer(xt, seq, norm_mix[layer], sh1, sc1, g1, sgu_w_in[j], sgu_b_in[j],
                            sgu_ln_g[j], sgu_ln_b[j], sgu_w_s[j], sgu_b_s[j], sgu_w_out[j])
        xt = _peer_layer(xt, seq, norm_ffn[layer], sh2, sc2, g2, peer_w_query[layer], peer_sub_keys[layer],
                         u_all, v_all, layer, norm_final, layer == depth - 1)
    return xt.reshape(bsz, seq, d)
```
